```python
import jax, jax.numpy as jnp
from jax import lax
import numpy as np

D_MODEL = 1024
BATCH = 16
SEQ = 2048
DEPTH = 2
DEC_BATCH = 32
DEC_SEQ = 1
PAST_LEN = 16384
PAGE_SIZE = 128

HEAD_DIM = 64
N_HEADS = D_MODEL // HEAD_DIM
N_SB_HEADS = N_HEADS // 2
N_FOX_HEADS = N_HEADS - N_SB_HEADS
MIX_WIDTH = N_HEADS * HEAD_DIM
SB_WIDTH = N_SB_HEADS * HEAD_DIM
FOX_WIDTH = N_FOX_HEADS * HEAD_DIM
IN_COLS = 3 * MIX_WIDTH + N_FOX_HEADS
D_FF = ((8 * D_MODEL // 3 + 127) // 128) * 128
CONV_W = 3
PLE_DIM = 256
BLOCK_Q = 128
FORGET_BIAS = 2.0
EPS = 1e-6

kernel_name = 'hymba_style_stickbreaking_fox_convffn_step'


def _rmsnorm(x, g):
    xf = x.astype(jnp.float32)
    y = xf * lax.rsqrt(jnp.mean(xf * xf, axis=-1, keepdims=True) + EPS)
    return (y * g.astype(jnp.float32)).astype(x.dtype)


def _head_norm(o, g):
    of = o.astype(jnp.float32)
    y = of * lax.rsqrt(jnp.mean(of * of, axis=-1, keepdims=True) + EPS)
    y = y * g.reshape(o.shape[2], o.shape[3]).astype(jnp.float32)
    return y.astype(o.dtype)


def _stick_breaking(q_pos, q, k, v, k_pos):
    z = jnp.einsum('bqhd,bkhd->bhqk', q, k, preferred_element_type=jnp.float32) * (HEAD_DIM ** -0.5)
    mask = k_pos[None, :] < q_pos[:, None]
    log_1m = jnp.where(mask, jax.nn.log_sigmoid(-z), 0.0)
    suffix = lax.cumsum(log_1m, axis=3, reverse=True) - log_1m
    w = jnp.where(mask, jnp.exp(jax.nn.log_sigmoid(z) + suffix), 0.0)
    return jnp.einsum('bhqk,bkhd->bqhd', w.astype(v.dtype), v)


def _forgetting_attn(q_pos, q, cum_q, k, v, cum_k, k_pos):
    z = jnp.einsum('bqhd,bkhd->bhqk', q, k, preferred_element_type=jnp.float32) * (HEAD_DIM ** -0.5)
    bias = jnp.swapaxes(cum_q, 1, 2)[:, :, :, None] - jnp.swapaxes(cum_k, 1, 2)[:, :, None, :]
    mask = k_pos[None, :] <= q_pos[:, None]
    w = jax.nn.softmax(jnp.where(mask, z + bias, -jnp.inf), axis=-1)
    return jnp.einsum('bhqk,bkhd->bqhd', w.astype(v.dtype), v)


def _sweep_query_blocks(fn, q_pos, *q_args):
    S = q_pos.shape[0]
    if S <= BLOCK_Q or S % BLOCK_Q != 0:
        return fn(q_pos, *q_args)
    nb = S // BLOCK_Q

    def to_blocks(a):
        return jnp.moveaxis(a.reshape(a.shape[0], nb, BLOCK_Q, *a.shape[2:]), 1, 0)

    out = lax.map(lambda args: fn(*args),
                  (q_pos.reshape(nb, BLOCK_Q),) + tuple(to_blocks(a) for a in q_args))
    out = jnp.moveaxis(out, 0, 1)
    return out.reshape(out.shape[0], S, *out.shape[3:])


def _layer(x, p, past_k, past_v, past_logf, conv_buf,
           norm_attn, w_in, b_f, g_sb, g_fox, w_o,
           norm_ffn, w_up, w_conv, b_conv, w_down,
           norm_ple, w_ple_gate, w_ple_proj):
    B, S, _ = x.shape
    P = past_k.shape[1]
    h = _rmsnorm(x, norm_attn)
    proj = h @ w_in
    q = proj[..., :MIX_WIDTH].reshape(B, S, N_HEADS, HEAD_DIM)
    k = proj[..., MIX_WIDTH:2 * MIX_WIDTH].reshape(B, S, N_HEADS, HEAD_DIM)
    v = proj[..., 2 * MIX_WIDTH:3 * MIX_WIDTH].reshape(B, S, N_HEADS, HEAD_DIM)
    logf = jax.nn.log_sigmoid((proj[..., 3 * MIX_WIDTH:] + b_f).astype(jnp.float32))

    k_all = jnp.concatenate([past_k.astype(k.dtype), k], axis=1)
    v_all = jnp.concatenate([past_v.astype(v.dtype), v], axis=1)
    cum = lax.cumsum(jnp.concatenate([past_logf.astype(jnp.float32), logf], axis=1), axis=1)
    k_pos = jnp.arange(P + S)
    q_pos = P + jnp.arange(S)

    k_sb, v_sb = k_all[:, :, :N_SB_HEADS], v_all[:, :, :N_SB_HEADS]
    k_fx, v_fx = k_all[:, :, N_SB_HEADS:], v_all[:, :, N_SB_HEADS:]
    o_sb = _sweep_query_blocks(
        lambda qp, qb: _stick_breaking(qp, qb, k_sb, v_sb, k_pos),
        q_pos, q[:, :, :N_SB_HEADS])
    o_fx = _sweep_query_blocks(
        lambda qp, qb, cb: _forgetting_attn(qp, qb, cb, k_fx, v_fx, cum, k_pos),
        q_pos, q[:, :, N_SB_HEADS:], cum[:, P:])
    mix = jnp.concatenate([_head_norm(o_sb, g_sb).reshape(B, S, SB_WIDTH),
                           _head_norm(o_fx, g_fox).reshape(B, S, FOX_WIDTH)], axis=-1)
    x = x + mix @ w_o

    up = _rmsnorm(x, norm_ffn) @ w_up
    up_all = jnp.concatenate([conv_buf.astype(up.dtype), up], axis=1)
    conv = b_conv + sum(w_conv[j] * up_all[:, j:j + S] for j in range(CONV_W))
    a, g = jnp.split(conv, 2, axis=-1)
    x = x + (jax.nn.silu(g) * a) @ w_down
    new_buf = up_all[:, up_all.shape[1] - (CONV_W - 1):]

    gate = jax.nn.sigmoid(_rmsnorm(x, norm_ple) @ w_ple_gate)
    x = x + gate * (p @ w_ple_proj)
    return x, k, v, logf, new_buf


def setup_inputs(seed: int = 0) -> dict:
    key = jax.random.key(seed)
    ks = jax.random.split(key, 26)
    f32 = jnp.float32
    n_pages = PAST_LEN // PAGE_SIZE
    n_used = DEC_BATCH * n_pages
    n_pool = n_used + max(1, n_used // 4)
    nrm = lambda k, shape, s=1.0: jax.random.normal(k, shape, f32) * s
    page_table = jax.random.permutation(ks[6], n_pool)[:n_used].reshape(DEC_BATCH, n_pages).astype(jnp.int32)
    return {
        'x_prompt': nrm(ks[0], (BATCH, SEQ, D_MODEL)),
        'x_sample': nrm(ks[1], (DEC_BATCH, DEC_SEQ, D_MODEL)),
        'cache_k': nrm(ks[2], (DEPTH, n_pool, PAGE_SIZE, N_HEADS, HEAD_DIM)),
        'cache_v': nrm(ks[3], (DEPTH, n_pool, PAGE_SIZE, N_HEADS, HEAD_DIM)),
        'cache_logf': jax.nn.log_sigmoid(FORGET_BIAS + nrm(ks[4], (DEPTH, n_pool, PAGE_SIZE, N_FOX_HEADS))),
        'state_conv': nrm(ks[5], (DEPTH, DEC_BATCH, CONV_W - 1, 2 * D_FF)),
        'page_table': page_table,
        'p_prompt': nrm(ks[7], (DEPTH, BATCH, SEQ, PLE_DIM)),
        'p_sample': nrm(ks[8], (DEPTH, DEC_BATCH, DEC_SEQ, PLE_DIM)),
        'norm_attn': 1.0 + nrm(ks[9], (DEPTH, D_MODEL), 0.02),
        'w_in': nrm(ks[10], (DEPTH, D_MODEL, IN_COLS), D_MODEL ** -0.5),
        'b_f': FORGET_BIAS + nrm(ks[11], (DEPTH, N_FOX_HEADS), 0.1),
        'g_sb': 1.0 + nrm(ks[12], (DEPTH, SB_WIDTH), 0.02),
        'g_fox': 1.0 + nrm(ks[13], (DEPTH, FOX_WIDTH), 0.02),
        'w_o': nrm(ks[14], (DEPTH, MIX_WIDTH, D_MODEL), MIX_WIDTH ** -0.5),
        'norm_ffn': 1.0 + nrm(ks[15], (DEPTH, D_MODEL), 0.02),
        'w_up': nrm(ks[16], (DEPTH, D_MODEL, 2 * D_FF), D_MODEL ** -0.5),
        'w_conv': nrm(ks[17], (DEPTH, CONV_W, 2 * D_FF), CONV_W ** -0.5),
        'b_conv': nrm(ks[18], (DEPTH, 2 * D_FF), 0.01),
        'w_down': nrm(ks[19], (DEPTH, D_FF, D_MODEL), D_FF ** -0.5),
        'norm_ple': 1.0 + nrm(ks[20], (DEPTH, D_MODEL), 0.02),
        'w_ple_gate': nrm(ks[21], (DEPTH, D_MODEL, D_MODEL), D_MODEL ** -0.5),
        'w_ple_proj': nrm(ks[22], (DEPTH, PLE_DIM, D_MODEL), PLE_DIM ** -0.5),
        'norm_final': 1.0 + nrm(ks[23], (D_MODEL,), 0.02),
    }


def reference(x_prompt, x_sample, cache_k, cache_v, cache_logf, state_conv, page_table,
              p_prompt, p_sample, norm_attn, w_in, b_f, g_sb, g_fox, w_o,
              norm_ffn, w_up, w_conv, b_conv, w_down, norm_ple, w_ple_gate, w_ple_proj,
              norm_final):
    n_b = x_prompt.shape[0]
    n_db = x_sample.shape[0]
    past_len = page_table.shape[1] * PAGE_SIZE
    h_p, h_s = x_prompt, x_sample
    kp, vp, lp, cp, ksm, vsm, lsm, csm = [], [], [], [], [], [], [], []
    for i in range(DEPTH):
        weights = (norm_attn[i], w_in[i], b_f[i], g_sb[i], g_fox[i], w_o[i],
                   norm_ffn[i], w_up[i], w_conv[i], b_conv[i], w_down[i],
                   norm_ple[i], w_ple_gate[i], w_ple_proj[i])
        empty_kv = jnp.zeros((n_b, 0, N_HEADS, HEAD_DIM), x_prompt.dtype)
        empty_lf = jnp.zeros((n_b, 0, N_FOX_HEADS), jnp.float32)
        zero_buf = jnp.zeros((n_b, CONV_W - 1, 2 * D_FF), x_prompt.dtype)
        h_p, k_new, v_new, lf_new, buf_new = _layer(h_p, p_prompt[i], empty_kv, empty_kv, empty_lf, zero_buf, *weights)
        kp.append(k_new); vp.append(v_new); lp.append(lf_new); cp.append(buf_new)
        past_k = cache_k[i, page_table].reshape(n_db, past_len, N_HEADS, HEAD_DIM)
        past_v = cache_v[i, page_table].reshape(n_db, past_len, N_HEADS, HEAD_DIM)
        past_lf = cache_logf[i, page_table].reshape(n_db, past_len, N_FOX_HEADS)
        h_s, k_new, v_new, lf_new, buf_new = _layer(h_s, p_sample[i], past_k, past_v, past_lf, state_conv[i], *weights)
        ksm.append(k_new); vsm.append(v_new); lsm.append(lf_new); csm.append(buf_new)
    y_prompt = _rmsnorm(h_p, norm_final)
    y_sample = _rmsnorm(h_s, norm_final)
    return (y_prompt, y_sample,
            jnp.stack(kp), jnp.stack(vp), jnp.stack(lp), jnp.stack(cp),
            jnp.stack(ksm), jnp.stack(vsm), jnp.stack(lsm), jnp.stack(csm))
```

```python
import functools

import jax
import jax.numpy as jnp
from jax import lax
from jax.experimental import pallas as pl
from jax.experimental.pallas import tpu as pltpu

F32 = jnp.float32
BF16 = jnp.bfloat16

D_MODEL = 1024
HEAD_DIM = 64
N_HEADS = 16
N_SB_HEADS = 8
N_FOX_HEADS = 8
SB_WIDTH = N_SB_HEADS * HEAD_DIM
D_FF = 2816
PLE_DIM = 256
PAGE_SIZE = 128
CONV_W = 3
EPS = 1e-6
SCALE = HEAD_DIM ** -0.5

LANES = 128
HEADS_PER_BLOCK = LANES // HEAD_DIM
VMEM_LIMIT = 56 * 1024 * 1024
NEG_BIG = -1e30

_NT = (((1,), (1,)), ((), ()))


def _softplus(z):
    return jnp.maximum(z, 0.0) + jnp.log1p(jnp.exp(-jnp.abs(z)))


def _sigmoid(z):
    return 1.0 / (1.0 + jnp.exp(-z))


def _rms(x, g):
    y = x * lax.rsqrt(jnp.mean(x * x, axis=-1, keepdims=True) + EPS)
    return y * g


def _split(x, parts):
    out = []
    r = x
    for _ in range(parts - 1):
        hi = r.astype(BF16)
        out.append(hi)
        r = r - hi.astype(F32)
    out.append(r.astype(BF16))
    return out


def _tri_dot(x, u, parts):
    acc = None
    for p in _split(x, parts):
        d = jnp.dot(p, u, preferred_element_type=F32)
        acc = d if acc is None else acc + d
    return acc


def _tri(n, cmp):
    r = lax.broadcasted_iota(jnp.int32, (n, n), 0)
    c = lax.broadcasted_iota(jnp.int32, (n, n), 1)
    return jnp.where(cmp(r, c), 1.0, 0.0).astype(BF16)


def _params(sem):
    return pltpu.CompilerParams(dimension_semantics=sem, vmem_limit_bytes=VMEM_LIMIT)


def _resident():
    return pl.BlockSpec(memory_space=pltpu.VMEM)


def _inproj_body(x_ref, g_ref, w_ref, wft_ref, bf_ref,
                 q_ref, k_ref, v_ref, kb_ref, vb_ref, lft_ref):
    h = _rms(x_ref[...], g_ref[...]).astype(BF16)
    q = jnp.dot(h, w_ref[:, 0:D_MODEL], preferred_element_type=F32)
    q_ref[...] = (q * SCALE).astype(BF16)
    k = jnp.dot(h, w_ref[:, D_MODEL:2 * D_MODEL], preferred_element_type=F32)
    k_ref[...] = k
    kb_ref[...] = k.astype(BF16)
    v = jnp.dot(h, w_ref[:, 2 * D_MODEL:3 * D_MODEL], preferred_element_type=F32)
    v_ref[...] = v
    vb_ref[...] = v.astype(BF16)
    zf = lax.dot_general(wft_ref[...], h, _NT, preferred_element_type=F32) + bf_ref[...]
    lft_ref[...] = -_softplus(-zf)


def _inproj(x, g, w_qkv, wft, bf, ts):
    t = x.shape[0]
    row = lambda i: (i, 0)
    tok = lambda dt: jax.ShapeDtypeStruct((t, D_MODEL), dt)
    return pl.pallas_call(
        _inproj_body,
        grid=(t // ts,),
        in_specs=[pl.BlockSpec((ts, D_MODEL), row), _resident(), _resident(),
                  _resident(), _resident()],
        out_specs=[pl.BlockSpec((ts, D_MODEL), row)] * 5
        + [pl.BlockSpec((N_FOX_HEADS, ts), lambda i: (0, i))],
        out_shape=[tok(BF16), tok(F32), tok(F32), tok(BF16), tok(BF16),
                   jax.ShapeDtypeStruct((N_FOX_HEADS, t), F32)],
        compiler_params=_params(("parallel",)),
        name="inproj",
    )(x, g, w_qkv, wft, bf)


def _cumsum_body(x_ref, o_ref, *, seq):
    u = _tri(LANES, lambda r, c: r <= c)
    carry = jnp.zeros((N_FOX_HEADS, 1), F32)
    for c in range(seq // LANES):
        cols = slice(c * LANES, (c + 1) * LANES)
        y = _tri_dot(x_ref[:, cols], u, 3) + carry
        o_ref[:, cols] = y
        carry = y[:, LANES - 1:LANES]


def _cumsum(lft, seq):
    t = lft.shape[1]
    spec = pl.BlockSpec((N_FOX_HEADS, seq), lambda b: (0, b))
    return pl.pallas_call(
        functools.partial(_cumsum_body, seq=seq),
        grid=(t // seq,),
        in_specs=[spec],
        out_specs=spec,
        out_shape=jax.ShapeDtypeStruct(lft.shape, F32),
        compiler_params=_params(("parallel",)),
        name="logf_cumsum",
    )(lft)


def _head_pair(q):
    first = lax.broadcasted_iota(jnp.int32, q.shape, 1) < HEAD_DIM
    qf = q.astype(F32)
    return first, (jnp.where(first, qf, 0.0).astype(BF16), jnp.where(first, 0.0, qf).astype(BF16))


def _pair_head_norm(first, o0, o1, g):
    o = jnp.where(first, o0, o1)
    o2 = o * o
    ss0 = jnp.sum(jnp.where(first, o2, 0.0), axis=-1, keepdims=True)
    ss1 = jnp.sum(jnp.where(first, 0.0, o2), axis=-1, keepdims=True)
    ms = jnp.where(first, ss0, ss1) * (1.0 / HEAD_DIM)
    return o * lax.rsqrt(ms + EPS) * g


def _sb_attn_body(q_ref, k_ref, v_ref, g_ref, o_ref, *, tq):
    i = pl.program_id(2)
    first, qh = _head_pair(q_ref[...])
    u = _tri(tq, lambda r, c: r > c)
    below = (lax.broadcasted_iota(jnp.int32, (tq, tq), 1)
             < lax.broadcasted_iota(jnp.int32, (tq, tq), 0))

    def block(j, state, diagonal):
        rows = pl.ds(pl.multiple_of(j * tq, tq), tq)
        k = k_ref[rows, :]
        v = v_ref[rows, :]
        new = []
        for h in range(HEADS_PER_BLOCK):
            carry, acc = state[h]
            z = lax.dot_general(qh[h], k, _NT, preferred_element_type=F32)
            sp = _softplus(z)
            l1m = jnp.where(below, -sp, 0.0) if diagonal else -sp
            w = jnp.exp(z - sp + _tri_dot(l1m, u, 2) + carry)
            if diagonal:
                w = jnp.where(below, w, 0.0)
            acc = acc + jnp.dot(w.astype(BF16), v, preferred_element_type=F32)
            carry = carry + jnp.sum(l1m, axis=-1, keepdims=True)
            new.append((carry, acc))
        return tuple(new)

    zero = (jnp.zeros((tq, 1), F32), jnp.zeros((tq, LANES), F32))
    state = block(i, (zero,) * HEADS_PER_BLOCK, True)
    state = lax.fori_loop(0, i, lambda jj, st: block(i - 1 - jj, st, False), state)
    o_ref[...] = _pair_head_norm(first, state[0][1], state[1][1], g_ref[...]).astype(BF16)


def _fox_attn_body(q_ref, k_ref, v_ref, cq_ref, ck_ref, g_ref, o_ref, *, tq):
    hp = pl.program_id(1)
    i = pl.program_id(2)
    first, qh = _head_pair(q_ref[...])
    cq_all = cq_ref[...]
    head_lane = lax.broadcasted_iota(jnp.int32, cq_all.shape, 1)
    cq = [jnp.sum(jnp.where(head_lane == HEADS_PER_BLOCK * hp + h, cq_all, 0.0),
                  axis=-1, keepdims=True) for h in range(HEADS_PER_BLOCK)]
    upto = (lax.broadcasted_iota(jnp.int32, (tq, tq), 1)
            <= lax.broadcasted_iota(jnp.int32, (tq, tq), 0))

    def block(j, state, diagonal):
        rows = pl.ds(pl.multiple_of(j * tq, tq), tq)
        k = k_ref[rows, :]
        v = v_ref[rows, :]
        new = []
        for h in range(HEADS_PER_BLOCK):
            m, l, acc = state[h]
            z = lax.dot_general(qh[h], k, _NT, preferred_element_type=F32)
            z = z + (cq[h] - ck_ref[h, j])
            if diagonal:
                z = jnp.where(upto, z, NEG_BIG)
            m_new = jnp.maximum(m, jnp.max(z, axis=-1, keepdims=True))
            alpha = jnp.exp(m - m_new)
            p = jnp.exp(z - m_new)
            l = alpha * l + jnp.sum(p, axis=-1, keepdims=True)
            acc = alpha * acc + jnp.dot(p.astype(BF16), v, preferred_element_type=F32)
            new.append((m_new, l, acc))
        return tuple(new)

    zero = (jnp.full((tq, 1), NEG_BIG, F32), jnp.zeros((tq, 1), F32), jnp.zeros((tq, LANES), F32))
    state = block(i, (zero,) * HEADS_PER_BLOCK, True)
    state = lax.fori_loop(0, i, lambda jj, st: block(i - 1 - jj, st, False), state)
    o0 = state[0][2] / state[0][1]
    o1 = state[1][2] / state[1][1]
    o_ref[...] = _pair_head_norm(first, o0, o1, g_ref[...]).astype(BF16)


def _prompt_attention(q, kb, vb, cq, ck, g_sb, g_fox, n_seq, seq, tq):
    t = q.shape[0]
    nq = seq // tq
    n_blk = SB_WIDTH // LANES
    grid = (n_seq, n_blk, nq)
    out = jax.ShapeDtypeStruct((t, SB_WIDTH), BF16)
    sem = ("parallel", "parallel", "arbitrary")

    def specs(off):
        qs = pl.BlockSpec((tq, LANES), lambda b, hp, i: (b * nq + i, hp + off))
        kvs = pl.BlockSpec((seq, LANES), lambda b, hp, i: (b, hp + off))
        return qs, kvs

    gain = pl.BlockSpec((1, LANES), lambda b, hp, i: (0, hp))
    o_spec = pl.BlockSpec((tq, LANES), lambda b, hp, i: (b * nq + i, hp))

    qs, kvs = specs(0)
    o_sb = pl.pallas_call(
        functools.partial(_sb_attn_body, tq=tq), grid=grid,
        in_specs=[qs, kvs, kvs, gain], out_specs=o_spec, out_shape=out,
        compiler_params=_params(sem), name="sb_attention",
    )(q, kb, vb, g_sb)

    qs, kvs = specs(n_blk)
    cq_spec = pl.BlockSpec((tq, N_FOX_HEADS), lambda b, hp, i: (b * nq + i, 0))
    ck_spec = pl.BlockSpec((HEADS_PER_BLOCK, nq, 1, tq), lambda b, hp, i: (hp, b, 0, 0))
    o_fox = pl.pallas_call(
        functools.partial(_fox_attn_body, tq=tq), grid=grid,
        in_specs=[qs, kvs, kvs, cq_spec, ck_spec, gain], out_specs=o_spec, out_shape=out,
        compiler_params=_params(sem), name="fox_attention",
    )(q, kb, vb, cq, ck, g_fox)
    return o_sb, o_fox


def _paged_attn_body(pt_ref, q_ref, kn_ref, vn_ref, lfn_ref, g_ref, *refs, pages_per_step):
    del pt_ref
    npg = pages_per_step
    k_refs = refs[0:npg]
    v_refs = refs[npg:2 * npg]
    lf_refs = refs[2 * npg:3 * npg]
    o_ref = refs[3 * npg]
    acc_ref, m_ref, l_ref, csb_ref, cfx_ref = refs[3 * npg + 1:]
    step = pl.program_id(1)

    row = lax.broadcasted_iota(jnp.int32, (N_HEADS, D_MODEL), 0)
    col = lax.broadcasted_iota(jnp.int32, (N_HEADS, D_MODEL), 1)
    own = (col >= row * HEAD_DIM) & (col < (row + 1) * HEAD_DIM)
    qf = jnp.where(own, q_ref[0].astype(F32), 0.0)
    qmat = qf.astype(BF16)
    u = _tri(PAGE_SIZE, lambda r, c: r > c)
    is_fox = lax.broadcasted_iota(jnp.int32, (N_HEADS, 1), 0) >= N_SB_HEADS

    @pl.when(step == 0)
    def _():
        z_self = jnp.sum(qf * kn_ref[0], axis=-1, keepdims=True)
        m_ref[...] = z_self[N_SB_HEADS:]
        l_ref[...] = jnp.ones((N_FOX_HEADS, 1), F32)
        acc_ref[...] = jnp.where(is_fox, jnp.broadcast_to(vn_ref[0], (N_HEADS, D_MODEL)), 0.0)
        csb_ref[...] = jnp.zeros((N_SB_HEADS, 1), F32)
        cfx_ref[...] = lfn_ref[0]

    for p in reversed(range(npg)):
        z = lax.dot_general(qmat, k_refs[p][0, 0].astype(BF16), _NT,
                            preferred_element_type=F32)
        z_sb = z[:N_SB_HEADS]
        z_fx = z[N_SB_HEADS:]
        sp = _softplus(z_sb)
        l1m = -sp
        w_sb = jnp.exp(z_sb - sp + _tri_dot(l1m, u, 2) + csb_ref[...])
        csb_ref[...] += jnp.sum(l1m, axis=-1, keepdims=True)

        lf = lf_refs[p][0, 0]
        logit = z_fx + _tri_dot(lf, u, 3) + cfx_ref[...]
        cfx_ref[...] += jnp.sum(lf, axis=-1, keepdims=True)
        m_old = m_ref[...]
        m_new = jnp.maximum(m_old, jnp.max(logit, axis=-1, keepdims=True))
        alpha = jnp.exp(m_old - m_new)
        pr = jnp.exp(logit - m_new)
        l_ref[...] = alpha * l_ref[...] + jnp.sum(pr, axis=-1, keepdims=True)
        m_ref[...] = m_new

        w = jnp.concatenate([w_sb, pr], axis=0).astype(BF16)
        pv = jnp.dot(w, v_refs[p][0, 0].astype(BF16), preferred_element_type=F32)
        rescale = jnp.concatenate([jnp.ones((N_SB_HEADS, 1), F32), alpha], axis=0)
        acc_ref[...] = acc_ref[...] * rescale + pv

    @pl.when(step == pl.num_programs(1) - 1)
    def _():
        denom = jnp.concatenate([jnp.ones((N_SB_HEADS, 1), F32), l_ref[...]], axis=0)
        o = jnp.where(own, acc_ref[...] / denom, 0.0)
        ms = jnp.sum(o * o, axis=-1, keepdims=True) * (1.0 / HEAD_DIM)
        y = o * lax.rsqrt(ms + EPS)
        o_ref[0] = jnp.sum(y, axis=0, keepdims=True) * g_ref[...]


def _paged_attention(layer, page_table, q, k_new, v_new, lf_new, gains,
                     cache_k, cache_v, cache_lft, pages_per_step):
    n_db, n_pages = page_table.shape
    npg = pages_per_step
    n_steps = n_pages // npg
    row3 = lambda b, s, pt: (b, 0, 0)

    def page(p, shape):
        return pl.BlockSpec(
            (1, 1) + shape,
            lambda b, s, pt: (layer, pt[b, (n_steps - 1 - s) * npg + p], 0, 0))

    in_specs = [pl.BlockSpec((1, 1, D_MODEL), row3), pl.BlockSpec((1, 1, D_MODEL), row3),
                pl.BlockSpec((1, 1, D_MODEL), row3), pl.BlockSpec((1, N_FOX_HEADS, 1), row3),
                pl.BlockSpec((1, D_MODEL), lambda b, s, pt: (0, 0))]
    in_specs += [page(p, (PAGE_SIZE, D_MODEL)) for p in range(npg)]
    in_specs += [page(p, (PAGE_SIZE, D_MODEL)) for p in range(npg)]
    in_specs += [page(p, (N_FOX_HEADS, PAGE_SIZE)) for p in range(npg)]
    grid_spec = pltpu.PrefetchScalarGridSpec(
        num_scalar_prefetch=1, grid=(n_db, n_steps), in_specs=in_specs,
        out_specs=pl.BlockSpec((1, 1, D_MODEL), row3),
        scratch_shapes=[pltpu.VMEM((N_HEADS, D_MODEL), F32),
                        pltpu.VMEM((N_FOX_HEADS, 1), F32), pltpu.VMEM((N_FOX_HEADS, 1), F32),
                        pltpu.VMEM((N_SB_HEADS, 1), F32), pltpu.VMEM((N_FOX_HEADS, 1), F32)])
    return pl.pallas_call(
        functools.partial(_paged_attn_body, pages_per_step=npg),
        grid_spec=grid_spec,
        out_shape=jax.ShapeDtypeStruct((n_db, 1, D_MODEL), F32),
        compiler_params=_params(("parallel", "arbitrary")),
        name="paged_attention",
    )(page_table, q, k_new, v_new, lf_new, gains,
      *([cache_k] * npg), *([cache_v] * npg), *([cache_lft] * npg))


def _ffn_body(*refs, ts, tiles_per_seq, decode, final, fc):
    if decode:
        (x_ref, msb_ref, mfx_ref, p_ref, h0_ref, h1_ref, wo_ref, nf_ref, wup_ref, wc_ref, bc_ref,
         wd_ref, npl_ref, wg_ref, wp_ref, nfin_ref, out_ref, up_ref) = refs
    else:
        (x_ref, msb_ref, mfx_ref, p_ref, wo_ref, nf_ref, wup_ref, wc_ref, bc_ref,
         wd_ref, npl_ref, wg_ref, wp_ref, nfin_ref, out_ref, tail_ref, carry_ref) = refs

        @pl.when(pl.program_id(0) % tiles_per_seq == 0)
        def _():
            carry_ref[...] = jnp.zeros(carry_ref.shape, F32)

    x1 = (x_ref[...]
          + jnp.dot(msb_ref[...], wo_ref[0:SB_WIDTH, :], preferred_element_type=F32)
          + jnp.dot(mfx_ref[...], wo_ref[SB_WIDTH:, :], preferred_element_type=F32))
    h2 = _rms(x1, nf_ref[...]).astype(BF16)
    row = lax.broadcasted_iota(jnp.int32, (ts, fc), 0)
    acc = jnp.zeros((ts, D_MODEL), F32)
    for c in range(D_FF // fc):
        halves = []
        for base in (0, D_FF):
            cols = slice(base + c * fc, base + (c + 1) * fc)
            up = jnp.dot(h2, wup_ref[:, cols], preferred_element_type=F32)
            if decode:
                sh2 = h0_ref[:, cols]
                sh1 = h1_ref[:, cols]
                up_ref[:, cols] = up
            else:
                c0 = carry_ref[0:1, cols]
                c1 = carry_ref[1:2, cols]
                sh1 = jnp.where(row == 0, c1, pltpu.roll(up, 1, 0))
                sh2 = jnp.where(row == 0, c0, jnp.where(row == 1, c1, pltpu.roll(up, 2, 0)))
                carry_ref[:, cols] = up[ts - 2:ts, :]
                tail_ref[0, :, cols] = up[ts - 2:ts, :]
            halves.append(bc_ref[:, cols] + wc_ref[0:1, cols] * sh2
                          + wc_ref[1:2, cols] * sh1 + wc_ref[2:3, cols] * up)
        a, g = halves
        y = (g * _sigmoid(g) * a).astype(BF16)
        acc = acc + jnp.dot(y, wd_ref[c * fc:(c + 1) * fc, :], preferred_element_type=F32)
    x2 = x1 + acc
    gate = _sigmoid(jnp.dot(_rms(x2, npl_ref[...]).astype(BF16), wg_ref[...],
                            preferred_element_type=F32))
    x3 = x2 + gate * jnp.dot(p_ref[...].astype(BF16), wp_ref[...], preferred_element_type=F32)
    out_ref[...] = _rms(x3, nfin_ref[...]) if final else x3


def _ffn(x, m_sb, m_fox, p, hist, w, nfin, *, ts, seq, final, fc=256):
    t = x.shape[0]
    decode = hist is not None
    row = lambda i: (i, 0)
    tok = lambda width: pl.BlockSpec((ts, width), row)
    acts = [x, m_sb, m_fox, p]
    act_specs = [tok(D_MODEL), tok(SB_WIDTH), tok(SB_WIDTH), tok(PLE_DIM)]
    if decode:
        acts += list(hist)
        act_specs += [tok(2 * D_FF), tok(2 * D_FF)]
        out_shape = [jax.ShapeDtypeStruct((t, D_MODEL), F32),
                     jax.ShapeDtypeStruct((t, 2 * D_FF), F32)]
        out_specs = [tok(D_MODEL), tok(2 * D_FF)]
        scratch = []
        tiles_per_seq = 1
    else:
        tiles_per_seq = seq // ts
        out_shape = [jax.ShapeDtypeStruct((t, D_MODEL), F32),
                     jax.ShapeDtypeStruct((t // seq, CONV_W - 1, 2 * D_FF), F32)]
        out_specs = [tok(D_MODEL),
                     pl.BlockSpec((1, CONV_W - 1, 2 * D_FF), lambda i: (i // tiles_per_seq, 0, 0))]
        scratch = [pltpu.VMEM((CONV_W - 1, 2 * D_FF), F32)]
    weights = [w["w_o"], w["norm_ffn"], w["w_up"], w["w_conv"], w["b_conv"], w["w_down"],
               w["norm_ple"], w["w_ple_gate"], w["w_ple_proj"], nfin]
    return pl.pallas_call(
        functools.partial(_ffn_body, ts=ts, tiles_per_seq=tiles_per_seq, decode=decode,
                          final=final, fc=fc),
        grid=(t // ts,),
        in_specs=act_specs + [_resident()] * len(weights),
        out_specs=out_specs, out_shape=out_shape, scratch_shapes=scratch,
        compiler_params=_params(("arbitrary",)),
        name="ffn_decode" if decode else "ffn_prompt",
    )(*acts, *weights)


def kernel(x_prompt, x_sample, cache_k, cache_v, cache_logf, state_conv, page_table, p_prompt,
           p_sample, norm_attn, w_in, b_f, g_sb, g_fox, w_o, norm_ffn, w_up, w_conv, b_conv,
           w_down, norm_ple, w_ple_gate, w_ple_proj, norm_final):
    n_b, seq, _ = x_prompt.shape
    n_db = x_sample.shape[0]
    depth = w_in.shape[0]
    n_pool = cache_k.shape[1]
    tq = 128
    nq = seq // tq
    t_p = n_b * seq

    h_p = x_prompt.reshape(t_p, D_MODEL)
    h_s = x_sample.reshape(n_db, D_MODEL)
    ck_pages = cache_k.reshape(depth, n_pool, PAGE_SIZE, D_MODEL)
    cv_pages = cache_v.reshape(depth, n_pool, PAGE_SIZE, D_MODEL)
    clf_pages = jnp.swapaxes(cache_logf, 2, 3)
    nfin = norm_final.reshape(1, D_MODEL)

    outs = {name: [] for name in ("kp", "vp", "lp", "cp", "ks", "vs", "ls", "cs")}
    for i in range(depth):
        last = i == depth - 1
        w_qkv = w_in[i, :, :3 * D_MODEL].astype(BF16)
        wft = w_in[i, :, 3 * D_MODEL:].T.astype(BF16)
        bf = b_f[i].reshape(N_FOX_HEADS, 1)
        na = norm_attn[i].reshape(1, D_MODEL)
        gsb = g_sb[i].reshape(1, SB_WIDTH)
        gfx = g_fox[i].reshape(1, SB_WIDTH)
        wl = {
            "w_o": w_o[i].astype(BF16), "norm_ffn": norm_ffn[i].reshape(1, D_MODEL),
            "w_up": w_up[i].astype(BF16), "w_conv": w_conv[i],
            "b_conv": b_conv[i].reshape(1, 2 * D_FF), "w_down": w_down[i].astype(BF16),
            "norm_ple": norm_ple[i].reshape(1, D_MODEL),
            "w_ple_gate": w_ple_gate[i].astype(BF16), "w_ple_proj": w_ple_proj[i].astype(BF16),
        }

        q, k, v, kb, vb, lft = _inproj(h_p, na, w_qkv, wft, bf, ts=512)
        ct = _cumsum(lft, seq)
        cq = ct.T
        ck = ct.reshape(N_FOX_HEADS, n_b * nq, 1, tq)
        o_sb, o_fox = _prompt_attention(q, kb, vb, cq, ck, gsb, gfx, n_b, seq, tq)
        h_p, tail = _ffn(h_p, o_sb, o_fox, p_prompt[i].reshape(t_p, PLE_DIM), None, wl, nfin,
                         ts=512, seq=seq, final=last)
        outs["kp"].append(k.reshape(n_b, seq, N_HEADS, HEAD_DIM))
        outs["vp"].append(v.reshape(n_b, seq, N_HEADS, HEAD_DIM))
        outs["lp"].append(lft.T.reshape(n_b, seq, N_FOX_HEADS))
        outs["cp"].append(tail)

        q, k, v, _, _, lft = _inproj(h_s, na, w_qkv, wft, bf, ts=n_db)
        mix = _paged_attention(
            i, page_table, q.reshape(n_db, 1, D_MODEL), k.reshape(n_db, 1, D_MODEL),
            v.reshape(n_db, 1, D_MODEL), lft.T.reshape(n_db, N_FOX_HEADS, 1),
            jnp.concatenate([gsb, gfx], axis=1), ck_pages, cv_pages, clf_pages, pages_per_step=4)
        mix = mix.reshape(n_db, D_MODEL).astype(BF16)
        hist = (state_conv[i, :, 0], state_conv[i, :, 1])
        h_s, up = _ffn(h_s, mix[:, :SB_WIDTH], mix[:, SB_WIDTH:], p_sample[i].reshape(n_db, PLE_DIM),
                       hist, wl, nfin, ts=n_db, seq=1, final=last)
        outs["ks"].append(k.reshape(n_db, 1, N_HEADS, HEAD_DIM))
        outs["vs"].append(v.reshape(n_db, 1, N_HEADS, HEAD_DIM))
        outs["ls"].append(lft.T.reshape(n_db, 1, N_FOX_HEADS))
        outs["cs"].append(jnp.stack([hist[1], up], axis=1))

    return (h_p.reshape(n_b, seq, D_MODEL), h_s.reshape(n_db, 1, D_MODEL),
            jnp.stack(outs["kp"]), jnp.stack(outs["vp"]), jnp.stack(outs["lp"]),
            jnp.stack(outs["cp"]), jnp.stack(outs["ks"]), jnp.stack(outs["vs"]),
            jnp.stack(outs["ls"]), jnp.stack(outs["cs"]))
```

```python
import functools

import jax
import jax.numpy as jnp
from jax import lax
from jax.experimental import pallas as pl
from jax.experimental.pallas import tpu as pltpu

F32 = jnp.float32
BF16 = jnp.bfloat16

D_MODEL = 1024
HEAD_DIM = 64
N_HEADS = 16
N_SB_HEADS = 8
N_FOX_HEADS = 8
SB_WIDTH = N_SB_HEADS * HEAD_DIM
D_FF = 2816
PLE_DIM = 256
PAGE_SIZE = 128
CONV_W = 3
EPS = 1e-6
SCALE = HEAD_DIM ** -0.5
LOG2E = 1.4426950408889634

LANES = 128
HEADS_PER_BLOCK = LANES // HEAD_DIM
VMEM_LIMIT = 56 * 1024 * 1024
NEG_BIG = -1e30

TOKEN_TILE = 512
ATTN_TILE = 512
PAGES_PER_STEP = 8
FF_CHUNK = 256

_NT = (((1,), (1,)), ((), ()))


def _softplus(z):
    return jnp.maximum(z, 0.0) + jnp.log1p(jnp.exp(-jnp.abs(z)))


def _sigmoid(z):
    return 1.0 / (1.0 + jnp.exp(-z))


def _rms(x, g):
    y = x * lax.rsqrt(jnp.mean(x * x, axis=-1, keepdims=True) + EPS)
    return y * g


def _split_cat(x, parts):
    out = []
    r = x
    for _ in range(parts - 1):
        hi = r.astype(BF16)
        out.append(hi)
        r = r - hi.astype(F32)
    out.append(r.astype(BF16))
    return jnp.concatenate(out, axis=1)


def _suffix_matrix(parts, sign):
    r = lax.broadcasted_iota(jnp.int32, (parts * LANES, 2 * LANES), 0) % LANES
    c = lax.broadcasted_iota(jnp.int32, (parts * LANES, 2 * LANES), 1)
    return jnp.where((c >= LANES) | (r > c), sign, 0.0).astype(BF16)


def _params(sem):
    return pltpu.CompilerParams(dimension_semantics=sem, vmem_limit_bytes=VMEM_LIMIT)


def _resident():
    return pl.BlockSpec(memory_space=pltpu.VMEM)


def _inproj_prompt_body(*refs):
    x_ref, g_ref, wq_ref, wkt_ref, wvt_ref, wft_ref, bf_ref = refs[:7]
    q_ref, kt_ref, vt_ref, kbt_ref, vbt_ref, lft_ref = refs[-6:]
    h = _rms(x_ref[...], g_ref[...]).astype(BF16)
    q = jnp.dot(h, wq_ref[...], preferred_element_type=F32)
    q_ref[...] = (q * (SCALE * LOG2E)).astype(BF16)
    kt = lax.dot_general(wkt_ref[...], h, _NT, preferred_element_type=F32)
    kt_ref[0, 0] = kt
    kbt_ref[0] = kt.astype(BF16)
    vt = lax.dot_general(wvt_ref[...], h, _NT, preferred_element_type=F32)
    vt_ref[0, 0] = vt
    vbt_ref[0] = vt.astype(BF16)
    zf = lax.dot_general(wft_ref[...], h, _NT, preferred_element_type=F32) + bf_ref[...]
    lft_ref[0] = -_softplus(-zf)


def _inproj_prompt(layer, depth, x, g, wq, wkt, wvt, wft, bf, kt_all, vt_all, n_seq, seq):
    ts = TOKEN_TILE
    tps = seq // ts
    stacked = jax.ShapeDtypeStruct((depth, n_seq, D_MODEL, seq), F32)
    cols = lambda dt: jax.ShapeDtypeStruct((n_seq, D_MODEL, seq), dt)
    ins = [x, g, wq, wkt, wvt, wft, bf]
    in_specs = [pl.BlockSpec((ts, D_MODEL), lambda b, i: (b * tps + i, 0))] + [_resident()] * 6
    aliases = {}
    if kt_all is not None:
        ins += [kt_all, vt_all]
        in_specs += [pl.BlockSpec(memory_space=pl.ANY)] * 2
        aliases = {7: 1, 8: 2}
    return pl.pallas_call(
        _inproj_prompt_body,
        grid=(n_seq, tps),
        in_specs=in_specs,
        out_specs=[pl.BlockSpec((ts, D_MODEL), lambda b, i: (b * tps + i, 0)),
                   pl.BlockSpec((1, 1, D_MODEL, ts), lambda b, i: (layer, b, 0, i)),
                   pl.BlockSpec((1, 1, D_MODEL, ts), lambda b, i: (layer, b, 0, i)),
                   pl.BlockSpec((1, D_MODEL, ts), lambda b, i: (b, 0, i)),
                   pl.BlockSpec((1, D_MODEL, ts), lambda b, i: (b, 0, i)),
                   pl.BlockSpec((1, N_FOX_HEADS, ts), lambda b, i: (b, 0, i))],
        out_shape=[jax.ShapeDtypeStruct((n_seq * seq, D_MODEL), BF16), stacked, stacked,
                   cols(BF16), cols(BF16),
                   jax.ShapeDtypeStruct((n_seq, N_FOX_HEADS, seq), F32)],
        input_output_aliases=aliases,
        compiler_params=_params(("parallel", "parallel")),
        name="inproj_prompt",
    )(*ins)


def _inproj_rows_body(x_ref, g_ref, wq_ref, wkt_ref, wvt_ref, wft_ref, bf_ref,
                      q_ref, k_ref, v_ref, lft_ref):
    h = _rms(x_ref[...], g_ref[...]).astype(BF16)
    q_ref[...] = (jnp.dot(h, wq_ref[...], preferred_element_type=F32) * SCALE).astype(BF16)
    k_ref[...] = lax.dot_general(h, wkt_ref[...], _NT, preferred_element_type=F32)
    v_ref[...] = lax.dot_general(h, wvt_ref[...], _NT, preferred_element_type=F32)
    zf = lax.dot_general(wft_ref[...], h, _NT, preferred_element_type=F32) + bf_ref[...]
    lft_ref[...] = -_softplus(-zf)


def _inproj_rows(x, g, wq, wkt, wvt, wft, bf):
    n = x.shape[0]
    tok = lambda dt: jax.ShapeDtypeStruct((n, D_MODEL), dt)
    return pl.pallas_call(
        _inproj_rows_body,
        in_specs=[_resident()] * 7,
        out_specs=[_resident()] * 4,
        out_shape=[tok(BF16), tok(F32), tok(F32), jax.ShapeDtypeStruct((N_FOX_HEADS, n), F32)],
        compiler_params=pltpu.CompilerParams(vmem_limit_bytes=VMEM_LIMIT),
        name="inproj_rows",
    )(x, g, wq, wkt, wvt, wft, bf)


def _cumsum_body(x_ref, o_ref, *, seq):
    r = lax.broadcasted_iota(jnp.int32, (3 * LANES, LANES), 0) % LANES
    c = lax.broadcasted_iota(jnp.int32, (3 * LANES, LANES), 1)
    u = jnp.where(r <= c, 1.0, 0.0).astype(BF16)
    carry = jnp.zeros((N_FOX_HEADS, 1), F32)
    for blk in range(seq // LANES):
        cols = slice(blk * LANES, (blk + 1) * LANES)
        y = jnp.dot(_split_cat(x_ref[0, :, cols], 3), u, preferred_element_type=F32) + carry
        o_ref[0, :, cols] = y * LOG2E
        carry = y[:, LANES - 1:LANES]


def _cumsum(lft):
    n_seq, _, seq = lft.shape
    spec = pl.BlockSpec((1, N_FOX_HEADS, seq), lambda b: (b, 0, 0))
    return pl.pallas_call(
        functools.partial(_cumsum_body, seq=seq),
        grid=(n_seq,), in_specs=[spec], out_specs=spec,
        out_shape=jax.ShapeDtypeStruct(lft.shape, F32),
        compiler_params=_params(("parallel",)),
        name="logf_cumsum",
    )(lft)


def _first_head_lanes(rows):
    return lax.broadcasted_iota(jnp.int32, (rows, LANES), 1) < HEAD_DIM


def _store_masked_q(q_ref, qm_ref, tq):
    first = _first_head_lanes(tq)
    qf = q_ref[...].astype(F32)
    qm_ref[0:tq, :] = jnp.where(first, qf, 0.0).astype(BF16)
    qm_ref[tq:2 * tq, :] = jnp.where(first, 0.0, qf).astype(BF16)


def _pair_head_norm(o0, o1, g):
    first = _first_head_lanes(o0.shape[0])
    o = jnp.where(first, o0, o1)
    o2 = o * o
    ss0 = jnp.sum(jnp.where(first, o2, 0.0), axis=-1, keepdims=True)
    ss1 = jnp.sum(jnp.where(first, 0.0, o2), axis=-1, keepdims=True)
    ms = jnp.where(first, ss0, ss1) * (1.0 / HEAD_DIM)
    return o * lax.rsqrt(ms + EPS) * g


def _key_chunk(ref, start):
    return ref[0, :, pl.ds(pl.multiple_of(start, LANES), LANES)]


def _diagonal_mask(rows, cmp):
    r = lax.broadcasted_iota(jnp.int32, (rows, LANES), 0)
    c = lax.broadcasted_iota(jnp.int32, (rows, LANES), 1)
    return (r >= LANES) | cmp(c, r)


def _sb_attn_body(q_ref, kt_ref, vt_ref, g_ref, o_ref, qm_ref, acc_ref, off_ref, *, tq):
    i = pl.program_id(2)
    n_groups = tq // LANES
    _store_masked_q(q_ref, qm_ref, tq)
    acc_ref[...] = jnp.zeros(acc_ref.shape, F32)
    off_ref[...] = jnp.zeros(off_ref.shape, F32)
    neg_suffix = _suffix_matrix(2, -1.0)

    def process(units):
        stage = []
        for rows, kt, _, _ in units:
            z = jnp.dot(qm_ref[rows, :], kt, preferred_element_type=F32)
            sp = jnp.maximum(z, 0.0) + jnp.log2(1.0 + jnp.exp2(-jnp.abs(z)))
            stage.append((z - sp, sp))
        sums = []
        for (_, sp), (_, _, _, mask) in zip(stage, units):
            if mask is not None:
                sp = jnp.where(mask, sp, 0.0)
            sums.append(jnp.dot(_split_cat(sp, 2), neg_suffix, preferred_element_type=F32))
        for (logsig, _), st, (rows, _, vt, mask) in zip(stage, sums, units):
            off = off_ref[rows, :]
            w = jnp.exp2(logsig + st[:, :LANES] + off)
            if mask is not None:
                w = jnp.where(mask, w, 0.0)
            acc_ref[rows, :] += lax.dot_general(w.astype(BF16), vt, _NT,
                                                preferred_element_type=F32)
            off_ref[rows, :] = off + st[:, LANES:]

    base = i * tq
    for c in reversed(range(n_groups)):
        kt = _key_chunk(kt_ref, base + c * LANES)
        vt = _key_chunk(vt_ref, base + c * LANES)
        mask = _diagonal_mask(tq - c * LANES, lambda col, row: col < row)
        process([(slice(h * tq + c * LANES, (h + 1) * tq), kt, vt, mask)
                 for h in range(HEADS_PER_BLOCK)])

    def earlier(jj, carry):
        units = []
        for d in range(1, n_groups + 1):
            start = base - (n_groups * jj + d) * LANES
            kt = _key_chunk(kt_ref, start)
            vt = _key_chunk(vt_ref, start)
            units += [(slice(h * tq, (h + 1) * tq), kt, vt, None) for h in range(HEADS_PER_BLOCK)]
        process(units)
        return carry

    lax.fori_loop(0, i, earlier, 0)
    y = _pair_head_norm(acc_ref[0:tq, :], acc_ref[tq:2 * tq, :], g_ref[...])
    o_ref[...] = y.astype(BF16)


def _fox_attn_body(q_ref, kt_ref, vt_ref, cq_ref, ck_ref, g_ref, o_ref,
                   qm_ref, acc_ref, m_ref, cqs_ref, *, tq):
    hp = pl.program_id(1)
    i = pl.program_id(2)
    n_groups = tq // LANES
    _store_masked_q(q_ref, qm_ref, tq)
    cq_all = cq_ref[...]
    head_lane = lax.broadcasted_iota(jnp.int32, cq_all.shape, 1)
    for h in range(HEADS_PER_BLOCK):
        cqs_ref[h * tq:(h + 1) * tq, :] = jnp.sum(
            jnp.where(head_lane == HEADS_PER_BLOCK * hp + h, cq_all, 0.0), axis=-1, keepdims=True)
    acc_ref[...] = jnp.zeros(acc_ref.shape, F32)
    m_ref[...] = jnp.full(m_ref.shape, NEG_BIG, F32)
    ones = jnp.ones((LANES, LANES), BF16)

    def load(start):
        j = start // LANES
        vt1 = jnp.concatenate([_key_chunk(vt_ref, start), ones], axis=0)
        return _key_chunk(kt_ref, start), vt1, (ck_ref[0, 0, j], ck_ref[0, 1, j])

    def process(jobs):
        scores = []
        for h, rows, chunks, mask in jobs:
            cq = cqs_ref[rows, :]
            zs = []
            for n, (kt, _, ck) in enumerate(chunks):
                z = jnp.dot(qm_ref[rows, :], kt, preferred_element_type=F32) + (cq - ck[h])
                zs.append(jnp.where(mask, z, NEG_BIG) if (n == 0 and mask is not None) else z)
            scores.append(zs)
        for zs, (h, rows, chunks, _) in zip(scores, jobs):
            m_old = m_ref[rows, :]
            m_new = jnp.maximum(m_old, jnp.max(functools.reduce(jnp.maximum, zs),
                                               axis=-1, keepdims=True))
            pv = None
            for z, (_, vt1, _) in zip(zs, chunks):
                d = lax.dot_general(jnp.exp2(z - m_new).astype(BF16), vt1, _NT,
                                    preferred_element_type=F32)
                pv = d if pv is None else pv + d
            acc_ref[rows, :] = acc_ref[rows, :] * jnp.exp2(m_old - m_new) + pv
            m_ref[rows, :] = m_new

    base = i * tq
    for c in reversed(range(n_groups)):
        chunk = load(base + c * LANES)
        mask = _diagonal_mask(tq - c * LANES, lambda col, row: col <= row)
        process([(h, slice(h * tq + c * LANES, (h + 1) * tq), [chunk], mask)
                 for h in range(HEADS_PER_BLOCK)])

    def earlier(jj, carry):
        chunks = [load(base - (n_groups * jj + d) * LANES) for d in range(1, n_groups + 1)]
        process([(h, slice(h * tq, (h + 1) * tq), chunks, None) for h in range(HEADS_PER_BLOCK)])
        return carry

    lax.fori_loop(0, i, earlier, 0)
    a0 = acc_ref[0:tq, :]
    a1 = acc_ref[tq:2 * tq, :]
    y = _pair_head_norm(a0[:, :LANES] / a0[:, LANES:], a1[:, :LANES] / a1[:, LANES:], g_ref[...])
    o_ref[...] = y.astype(BF16)


def _prompt_attention(q, kbt, vbt, cq, ck, g_sb, g_fox):
    n_seq, _, seq = kbt.shape
    tq = ATTN_TILE
    nq = seq // tq
    n_blk = SB_WIDTH // LANES
    rows = HEADS_PER_BLOCK * tq
    grid = (n_seq, n_blk, nq)
    out = jax.ShapeDtypeStruct((n_seq * seq, SB_WIDTH), BF16)
    sem = ("parallel", "parallel", "arbitrary")

    def specs(off):
        qs = pl.BlockSpec((tq, LANES), lambda b, hp, i: (b * nq + i, hp + off))
        kvs = pl.BlockSpec((1, LANES, seq), lambda b, hp, i: (b, hp + off, 0))
        return qs, kvs

    gain = pl.BlockSpec((1, LANES), lambda b, hp, i: (0, hp))
    o_spec = pl.BlockSpec((tq, LANES), lambda b, hp, i: (b * nq + i, hp))
    qm = pltpu.VMEM((rows, LANES), BF16)

    qs, kvs = specs(0)
    o_sb = pl.pallas_call(
        functools.partial(_sb_attn_body, tq=tq), grid=grid,
        in_specs=[qs, kvs, kvs, gain], out_specs=o_spec, out_shape=out,
        scratch_shapes=[qm, pltpu.VMEM((rows, LANES), F32), pltpu.VMEM((rows, LANES), F32)],
        compiler_params=_params(sem), name="sb_attention",
    )(q, kbt, vbt, g_sb)

    qs, kvs = specs(n_blk)
    cq_spec = pl.BlockSpec((tq, N_FOX_HEADS), lambda b, hp, i: (b * nq + i, 0))
    ck_spec = pl.BlockSpec((1, HEADS_PER_BLOCK, seq // LANES, 1, LANES),
                           lambda b, hp, i: (b, hp, 0, 0, 0))
    o_fox = pl.pallas_call(
        functools.partial(_fox_attn_body, tq=tq), grid=grid,
        in_specs=[qs, kvs, kvs, cq_spec, ck_spec, gain], out_specs=o_spec, out_shape=out,
        scratch_shapes=[qm, pltpu.VMEM((rows, 2 * LANES), F32),
                        pltpu.VMEM((rows, 1), F32), pltpu.VMEM((rows, 1), F32)],
        compiler_params=_params(sem), name="fox_attention",
    )(q, kbt, vbt, cq, ck, g_fox)
    return o_sb, o_fox


def _paged_attn_body(pt_ref, q_ref, kn_ref, vn_ref, lfn_ref, g_ref, *refs, npg):
    del pt_ref
    k_refs = refs[0:npg]
    v_refs = refs[npg:2 * npg]
    lf_refs = refs[2 * npg:3 * npg]
    o_ref = refs[3 * npg]
    acc_ref, m_ref, l_ref, osb_ref, ofx_ref = refs[3 * npg + 1:]
    step = pl.program_id(1)

    row = lax.broadcasted_iota(jnp.int32, (N_HEADS, D_MODEL), 0)
    col = lax.broadcasted_iota(jnp.int32, (N_HEADS, D_MODEL), 1)
    own = (col >= row * HEAD_DIM) & (col < (row + 1) * HEAD_DIM)
    qf = jnp.where(own, q_ref[0].astype(F32), 0.0)
    qmat = qf.astype(BF16)
    is_fox = lax.broadcasted_iota(jnp.int32, (N_HEADS, 1), 0) >= N_SB_HEADS

    @pl.when(step == 0)
    def _():
        z_self = jnp.sum(qf * kn_ref[0], axis=-1, keepdims=True)
        m_ref[...] = z_self[N_SB_HEADS:]
        l_ref[...] = jnp.ones((N_FOX_HEADS, 1), F32)
        acc_ref[...] = jnp.where(is_fox, jnp.broadcast_to(vn_ref[0], (N_HEADS, D_MODEL)), 0.0)
        osb_ref[...] = jnp.zeros((N_SB_HEADS, LANES), F32)
        ofx_ref[...] = jnp.broadcast_to(lfn_ref[0], (N_FOX_HEADS, LANES))

    kcat = jnp.concatenate([k_refs[p][0, 0].astype(BF16) for p in range(npg)], axis=1)
    z = jnp.dot(qmat, kcat, preferred_element_type=F32)
    z_sb = z[:N_SB_HEADS]
    sp = _softplus(z_sb)
    logsig = z_sb - sp
    neg_suffix = _suffix_matrix(2, -1.0)
    suffix = _suffix_matrix(3, 1.0)
    off_sb = osb_ref[...]
    off_fx = ofx_ref[...]
    w_sb = [None] * npg
    bias = [None] * npg
    for p in reversed(range(npg)):
        cols = slice(p * LANES, (p + 1) * LANES)
        st = jnp.dot(_split_cat(sp[:, cols], 2), neg_suffix, preferred_element_type=F32)
        w_sb[p] = jnp.exp(logsig[:, cols] + st[:, :LANES] + off_sb)
        off_sb = off_sb + st[:, LANES:]
        st = jnp.dot(_split_cat(lf_refs[p][0, 0], 3), suffix, preferred_element_type=F32)
        bias[p] = st[:, :LANES] + off_fx
        off_fx = off_fx + st[:, LANES:]
    osb_ref[...] = off_sb
    ofx_ref[...] = off_fx

    logit = z[N_SB_HEADS:] + jnp.concatenate(bias, axis=1)
    m_old = m_ref[...]
    m_new = jnp.maximum(m_old, jnp.max(logit, axis=-1, keepdims=True))
    alpha = jnp.exp(m_old - m_new)
    pr = jnp.exp(logit - m_new)
    l_ref[...] = alpha * l_ref[...] + jnp.sum(pr, axis=-1, keepdims=True)
    m_ref[...] = m_new

    w = jnp.concatenate([jnp.concatenate(w_sb, axis=1), pr], axis=0).astype(BF16)
    vcat = jnp.concatenate([v_refs[p][0, 0].astype(BF16) for p in range(npg)], axis=1)
    pv = lax.dot_general(w, vcat, _NT, preferred_element_type=F32)
    rescale = jnp.concatenate([jnp.ones((N_SB_HEADS, 1), F32), alpha], axis=0)
    acc_ref[...] = acc_ref[...] * rescale + pv

    @pl.when(step == pl.num_programs(1) - 1)
    def _():
        denom = jnp.concatenate([jnp.ones((N_SB_HEADS, 1), F32), l_ref[...]], axis=0)
        o = jnp.where(own, acc_ref[...] / denom, 0.0)
        ms = jnp.sum(o * o, axis=-1, keepdims=True) * (1.0 / HEAD_DIM)
        y = o * lax.rsqrt(ms + EPS)
        o_ref[0] = jnp.sum(y, axis=0, keepdims=True) * g_ref[...]


def _paged_attention(layer, page_table, q, k_new, v_new, lf_new, gains,
                     cache_kt, cache_vt, cache_lft):
    n_db, n_pages = page_table.shape
    npg = PAGES_PER_STEP
    n_steps = n_pages // npg
    row3 = lambda b, s, pt: (b, 0, 0)

    def page(p, shape):
        return pl.BlockSpec(
            (1, 1) + shape,
            lambda b, s, pt: (layer, pt[b, (n_steps - 1 - s) * npg + p], 0, 0))

    in_specs = [pl.BlockSpec((1, 1, D_MODEL), row3), pl.BlockSpec((1, 1, D_MODEL), row3),
                pl.BlockSpec((1, 1, D_MODEL), row3), pl.BlockSpec((1, N_FOX_HEADS, 1), row3),
                pl.BlockSpec((1, D_MODEL), lambda b, s, pt: (0, 0))]
    in_specs += [page(p, (D_MODEL, PAGE_SIZE)) for p in range(npg)]
    in_specs += [page(p, (D_MODEL, PAGE_SIZE)) for p in range(npg)]
    in_specs += [page(p, (N_FOX_HEADS, PAGE_SIZE)) for p in range(npg)]
    grid_spec = pltpu.PrefetchScalarGridSpec(
        num_scalar_prefetch=1, grid=(n_db, n_steps), in_specs=in_specs,
        out_specs=pl.BlockSpec((1, 1, D_MODEL), row3),
        scratch_shapes=[pltpu.VMEM((N_HEADS, D_MODEL), F32),
                        pltpu.VMEM((N_FOX_HEADS, 1), F32), pltpu.VMEM((N_FOX_HEADS, 1), F32),
                        pltpu.VMEM((N_SB_HEADS, LANES), F32), pltpu.VMEM((N_FOX_HEADS, LANES), F32)])
    return pl.pallas_call(
        functools.partial(_paged_attn_body, npg=npg),
        grid_spec=grid_spec,
        out_shape=jax.ShapeDtypeStruct((n_db, 1, D_MODEL), F32),
        compiler_params=_params(("parallel", "arbitrary")),
        name="paged_attention",
    )(page_table, q, k_new, v_new, lf_new, gains,
      *([cache_kt] * npg), *([cache_vt] * npg), *([cache_lft] * npg))


def _ffn_body(*refs, ts, tiles_per_seq, decode, final):
    fc = FF_CHUNK
    if decode:
        (x_ref, msb_ref, mfx_ref, p_ref, h0_ref, h1_ref, wo_ref, nf_ref, wup_ref, wc_ref, bc_ref,
         wd_ref, npl_ref, wg_ref, wp_ref, nfin_ref, out_ref, up_ref) = refs
    else:
        (x_ref, msb_ref, mfx_ref, p_ref, wo_ref, nf_ref, wup_ref, wc_ref, bc_ref,
         wd_ref, npl_ref, wg_ref, wp_ref, nfin_ref, out_ref, tail_ref, carry_ref) = refs

        @pl.when(pl.program_id(0) % tiles_per_seq == 0)
        def _():
            carry_ref[...] = jnp.zeros(carry_ref.shape, F32)

    x1 = (x_ref[...]
          + jnp.dot(msb_ref[...], wo_ref[0:SB_WIDTH, :], preferred_element_type=F32)
          + jnp.dot(mfx_ref[...], wo_ref[SB_WIDTH:, :], preferred_element_type=F32))
    h2 = _rms(x1, nf_ref[...]).astype(BF16)
    row = lax.broadcasted_iota(jnp.int32, (ts, fc), 0)
    acc = jnp.zeros((ts, D_MODEL), F32)
    for c in range(D_FF // fc):
        halves = []
        for base in (0, D_FF):
            cols = slice(base + c * fc, base + (c + 1) * fc)
            up = jnp.dot(h2, wup_ref[:, cols], preferred_element_type=F32)
            if decode:
                sh2 = h0_ref[:, cols]
                sh1 = h1_ref[:, cols]
                up_ref[:, cols] = up
            else:
                c0 = carry_ref[0:1, cols]
                c1 = carry_ref[1:2, cols]
                sh1 = jnp.where(row == 0, c1, pltpu.roll(up, 1, 0))
                sh2 = jnp.where(row == 0, c0, jnp.where(row == 1, c1, pltpu.roll(up, 2, 0)))
                carry_ref[:, cols] = up[ts - 2:ts, :]
                tail_ref[0, :, cols] = up[ts - 2:ts, :]
            halves.append(bc_ref[:, cols] + wc_ref[0:1, cols] * sh2
                          + wc_ref[1:2, cols] * sh1 + wc_ref[2:3, cols] * up)
        a, g = halves
        y = (g * _sigmoid(g) * a).astype(BF16)
        acc = acc + jnp.dot(y, wd_ref[c * fc:(c + 1) * fc, :], preferred_element_type=F32)
    x2 = x1 + acc
    gate = _sigmoid(jnp.dot(_rms(x2, npl_ref[...]).astype(BF16), wg_ref[...],
                            preferred_element_type=F32))
    x3 = x2 + gate * jnp.dot(p_ref[...].astype(BF16), wp_ref[...], preferred_element_type=F32)
    out_ref[...] = _rms(x3, nfin_ref[...]) if final else x3


def _ffn(x, m_sb, m_fox, p, hist, w, nfin, *, ts, seq, final):
    t = x.shape[0]
    decode = hist is not None
    row = lambda i: (i, 0)
    tok = lambda width: pl.BlockSpec((ts, width), row)
    acts = [x, m_sb, m_fox, p]
    act_specs = [tok(D_MODEL), tok(SB_WIDTH), tok(SB_WIDTH), tok(PLE_DIM)]
    if decode:
        acts += list(hist)
        act_specs += [tok(2 * D_FF), tok(2 * D_FF)]
        out_shape = [jax.ShapeDtypeStruct((t, D_MODEL), F32),
                     jax.ShapeDtypeStruct((t, 2 * D_FF), F32)]
        out_specs = [tok(D_MODEL), tok(2 * D_FF)]
        scratch = []
        tiles_per_seq = 1
    else:
        tiles_per_seq = seq // ts
        out_shape = [jax.ShapeDtypeStruct((t, D_MODEL), F32),
                     jax.ShapeDtypeStruct((t // seq, CONV_W - 1, 2 * D_FF), F32)]
        out_specs = [tok(D_MODEL),
                     pl.BlockSpec((1, CONV_W - 1, 2 * D_FF), lambda i: (i // tiles_per_seq, 0, 0))]
        scratch = [pltpu.VMEM((CONV_W - 1, 2 * D_FF), F32)]
    weights = [w["w_o"], w["norm_ffn"], w["w_up"], w["w_conv"], w["b_conv"], w["w_down"],
               w["norm_ple"], w["w_ple_gate"], w["w_ple_proj"], nfin]
    return pl.pallas_call(
        functools.partial(_ffn_body, ts=ts, tiles_per_seq=tiles_per_seq, decode=decode,
                          final=final),
        grid=(t // ts,),
        in_specs=act_specs + [_resident()] * len(weights),
        out_specs=out_specs, out_shape=out_shape, scratch_shapes=scratch,
        compiler_params=_params(("arbitrary",)),
        name="ffn_decode" if decode else "ffn_prompt",
    )(*acts, *weights)


def kernel(x_prompt, x_sample, cache_k, cache_v, cache_logf, state_conv, page_table, p_prompt,
           p_sample, norm_attn, w_in, b_f, g_sb, g_fox, w_o, norm_ffn, w_up, w_conv, b_conv,
           w_down, norm_ple, w_ple_gate, w_ple_proj, norm_final):
    n_b, seq, _ = x_prompt.shape
    n_db = x_sample.shape[0]
    depth = w_in.shape[0]
    n_pool = cache_k.shape[1]
    t_p = n_b * seq

    h_p = x_prompt.reshape(t_p, D_MODEL)
    h_s = x_sample.reshape(n_db, D_MODEL)
    ck_pages = jnp.transpose(cache_k, (0, 1, 3, 4, 2)).reshape(depth, n_pool, D_MODEL, PAGE_SIZE)
    cv_pages = jnp.transpose(cache_v, (0, 1, 3, 4, 2)).reshape(depth, n_pool, D_MODEL, PAGE_SIZE)
    clf_pages = jnp.swapaxes(cache_logf, 2, 3)
    nfin = norm_final.reshape(1, D_MODEL)

    kt_all = vt_all = None
    outs = {name: [] for name in ("lp", "cp", "ks", "vs", "ls", "cs")}
    for i in range(depth):
        last = i == depth - 1
        wq = w_in[i, :, :D_MODEL].astype(BF16)
        wkt = w_in[i, :, D_MODEL:2 * D_MODEL].T.astype(BF16)
        wvt = w_in[i, :, 2 * D_MODEL:3 * D_MODEL].T.astype(BF16)
        wft = w_in[i, :, 3 * D_MODEL:].T.astype(BF16)
        bf = b_f[i].reshape(N_FOX_HEADS, 1)
        na = norm_attn[i].reshape(1, D_MODEL)
        gsb = g_sb[i].reshape(1, SB_WIDTH)
        gfx = g_fox[i].reshape(1, SB_WIDTH)
        wl = {
            "w_o": w_o[i].astype(BF16), "norm_ffn": norm_ffn[i].reshape(1, D_MODEL),
            "w_up": w_up[i].astype(BF16), "w_conv": w_conv[i],
            "b_conv": b_conv[i].reshape(1, 2 * D_FF), "w_down": w_down[i].astype(BF16),
            "norm_ple": norm_ple[i].reshape(1, D_MODEL),
            "w_ple_gate": w_ple_gate[i].astype(BF16), "w_ple_proj": w_ple_proj[i].astype(BF16),
        }

        q, kt_all, vt_all, kbt, vbt, lft = _inproj_prompt(
            i, depth, h_p, na, wq, wkt, wvt, wft, bf, kt_all, vt_all, n_b, seq)
        ct = _cumsum(lft)
        cq = jnp.swapaxes(ct, 1, 2).reshape(t_p, N_FOX_HEADS)
        ck = ct.reshape(n_b, N_FOX_HEADS, seq // LANES, 1, LANES)
        o_sb, o_fox = _prompt_attention(q, kbt, vbt, cq, ck, gsb, gfx)
        h_p, tail = _ffn(h_p, o_sb, o_fox, p_prompt[i].reshape(t_p, PLE_DIM), None, wl, nfin,
                         ts=TOKEN_TILE, seq=seq, final=last)
        outs["lp"].append(lft)
        outs["cp"].append(tail)

        q, k, v, lft = _inproj_rows(h_s, na, wq, wkt, wvt, wft, bf)
        mix = _paged_attention(
            i, page_table, q.reshape(n_db, 1, D_MODEL), k.reshape(n_db, 1, D_MODEL),
            v.reshape(n_db, 1, D_MODEL), lft.T.reshape(n_db, N_FOX_HEADS, 1),
            jnp.concatenate([gsb, gfx], axis=1), ck_pages, cv_pages, clf_pages)
        mix = mix.reshape(n_db, D_MODEL).astype(BF16)
        hist = (state_conv[i, :, 0], state_conv[i, :, 1])
        h_s, up = _ffn(h_s, mix[:, :SB_WIDTH], mix[:, SB_WIDTH:], p_sample[i].reshape(n_db, PLE_DIM),
                       hist, wl, nfin, ts=n_db, seq=1, final=last)
        outs["ks"].append(k.reshape(n_db, 1, N_HEADS, HEAD_DIM))
        outs["vs"].append(v.reshape(n_db, 1, N_HEADS, HEAD_DIM))
        outs["ls"].append(lft.T.reshape(n_db, 1, N_FOX_HEADS))
        outs["cs"].append(jnp.stack([hist[1], up], axis=1))

    def heads_last(a):
        return jnp.transpose(a.reshape(depth, n_b, N_HEADS, HEAD_DIM, seq), (0, 1, 4, 2, 3))

    return (h_p.reshape(n_b, seq, D_MODEL), h_s.reshape(n_db, 1, D_MODEL),
            heads_last(kt_all), heads_last(vt_all),
            jnp.swapaxes(jnp.stack(outs["lp"]), 2, 3),
            jnp.stack(outs["cp"]), jnp.stack(outs["ks"]), jnp.stack(outs["vs"]),
            jnp.stack(outs["ls"]), jnp.stack(outs["cs"]))
```

```python
import functools

import jax
import jax.numpy as jnp
from jax import lax
from jax.experimental import pallas as pl
from jax.experimental.pallas import tpu as pltpu

F32 = jnp.float32
BF16 = jnp.bfloat16

D_MODEL = 1024
HEAD_DIM = 64
N_HEADS = 16
N_SB_HEADS = 8
N_FOX_HEADS = 8
SB_WIDTH = N_SB_HEADS * HEAD_DIM
D_FF = 2816
PLE_DIM = 256
PAGE_SIZE = 128
CONV_W = 3
EPS = 1e-6
SCALE = HEAD_DIM ** -0.5
LOG2E = 1.4426950408889634

LANES = 128
SUBLANES = 8
HEADS_PER_BLOCK = LANES // HEAD_DIM
VMEM_LIMIT = 56 * 1024 * 1024
NEG_BIG = -1e30

TOKEN_TILE = 512
ATTN_TILE = 512
PAGES_PER_STEP = 8
FF_CHUNK = 256

_NT = (((1,), (1,)), ((), ()))


def _softplus(z):
    return jnp.maximum(z, 0.0) + jnp.log1p(jnp.exp(-jnp.abs(z)))


def _sigmoid(z):
    return 1.0 / (1.0 + jnp.exp(-z))


def _rms(x, g):
    y = x * lax.rsqrt(jnp.mean(x * x, axis=-1, keepdims=True) + EPS)
    return y * g


def _split_cat(x, parts):
    out = []
    r = x
    for _ in range(parts - 1):
        hi = r.astype(BF16)
        out.append(hi)
        r = r - hi.astype(F32)
    out.append(r.astype(BF16))
    return jnp.concatenate(out, axis=1)


def _suffix_matrix(parts, sign, inclusive=False):
    r = lax.broadcasted_iota(jnp.int32, (parts * LANES, 2 * LANES), 0) % LANES
    c = lax.broadcasted_iota(jnp.int32, (parts * LANES, 2 * LANES), 1)
    later = (r >= c) if inclusive else (r > c)
    return jnp.where((c >= LANES) | later, sign, 0.0).astype(BF16)


def _softplus2(z):
    neg_abs = lax.bitcast_convert_type(
        lax.bitcast_convert_type(z, jnp.uint32) | jnp.uint32(0x80000000), F32)
    return jnp.maximum(z, 0.0) + jnp.log2(1.0 + jnp.exp2(neg_abs))


def _params(sem):
    return pltpu.CompilerParams(dimension_semantics=sem, vmem_limit_bytes=VMEM_LIMIT)


def _resident():
    return pl.BlockSpec(memory_space=pltpu.VMEM)


def _inproj_prompt_body(*refs):
    x_ref, g_ref, wq_ref, wkt_ref, wvt_ref, wft_ref, bf_ref = refs[:7]
    q_ref, kt_ref, vt_ref, kbt_ref, vbt_ref, lft_ref = refs[-6:]
    h = _rms(x_ref[...], g_ref[...]).astype(BF16)
    q = jnp.dot(h, wq_ref[...], preferred_element_type=F32)
    q_ref[...] = (q * (SCALE * LOG2E)).astype(BF16)
    kt = lax.dot_general(wkt_ref[...], h, _NT, preferred_element_type=F32)
    kt_ref[0, 0] = kt
    kbt_ref[0] = kt.astype(BF16)
    vt = lax.dot_general(wvt_ref[...], h, _NT, preferred_element_type=F32)
    vt_ref[0, 0] = vt
    vbt_ref[0] = vt.astype(BF16)
    zf = lax.dot_general(wft_ref[...], h, _NT, preferred_element_type=F32) + bf_ref[...]
    lft_ref[0] = -_softplus(-zf)


def _inproj_prompt(layer, depth, x, g, wq, wkt, wvt, wft, bf, kt_all, vt_all, n_seq, seq):
    ts = TOKEN_TILE
    tps = seq // ts
    stacked = jax.ShapeDtypeStruct((depth, n_seq, D_MODEL, seq), F32)
    cols = lambda dt: jax.ShapeDtypeStruct((n_seq, D_MODEL, seq), dt)
    ins = [x, g, wq, wkt, wvt, wft, bf]
    in_specs = [pl.BlockSpec((ts, D_MODEL), lambda b, i: (b * tps + i, 0))] + [_resident()] * 6
    aliases = {}
    if kt_all is not None:
        ins += [kt_all, vt_all]
        in_specs += [pl.BlockSpec(memory_space=pl.ANY)] * 2
        aliases = {7: 1, 8: 2}
    return pl.pallas_call(
        _inproj_prompt_body,
        grid=(n_seq, tps),
        in_specs=in_specs,
        out_specs=[pl.BlockSpec((ts, D_MODEL), lambda b, i: (b * tps + i, 0)),
                   pl.BlockSpec((1, 1, D_MODEL, ts), lambda b, i: (layer, b, 0, i)),
                   pl.BlockSpec((1, 1, D_MODEL, ts), lambda b, i: (layer, b, 0, i)),
                   pl.BlockSpec((1, D_MODEL, ts), lambda b, i: (b, 0, i)),
                   pl.BlockSpec((1, D_MODEL, ts), lambda b, i: (b, 0, i)),
                   pl.BlockSpec((1, N_FOX_HEADS, ts), lambda b, i: (b, 0, i))],
        out_shape=[jax.ShapeDtypeStruct((n_seq * seq, D_MODEL), BF16), stacked, stacked,
                   cols(BF16), cols(BF16),
                   jax.ShapeDtypeStruct((n_seq, N_FOX_HEADS, seq), F32)],
        input_output_aliases=aliases,
        compiler_params=_params(("parallel", "parallel")),
        name="inproj_prompt",
    )(*ins)


def _inproj_rows_body(x_ref, g_ref, wq_ref, wkt_ref, wvt_ref, wft_ref, bf_ref,
                      q_ref, k_ref, v_ref, lft_ref):
    h = _rms(x_ref[...], g_ref[...]).astype(BF16)
    q_ref[...] = (jnp.dot(h, wq_ref[...], preferred_element_type=F32) * SCALE).astype(BF16)
    k_ref[...] = lax.dot_general(h, wkt_ref[...], _NT, preferred_element_type=F32)
    v_ref[...] = lax.dot_general(h, wvt_ref[...], _NT, preferred_element_type=F32)
    zf = lax.dot_general(wft_ref[...], h, _NT, preferred_element_type=F32) + bf_ref[...]
    lft_ref[...] = -_softplus(-zf)


def _inproj_rows(x, g, wq, wkt, wvt, wft, bf):
    n = x.shape[0]
    tok = lambda dt: jax.ShapeDtypeStruct((n, D_MODEL), dt)
    return pl.pallas_call(
        _inproj_rows_body,
        in_specs=[_resident()] * 7,
        out_specs=[_resident()] * 4,
        out_shape=[tok(BF16), tok(F32), tok(F32), jax.ShapeDtypeStruct((N_FOX_HEADS, n), F32)],
        compiler_params=pltpu.CompilerParams(vmem_limit_bytes=VMEM_LIMIT),
        name="inproj_rows",
    )(x, g, wq, wkt, wvt, wft, bf)


def _cumsum_body(x_ref, o_ref, *, seq):
    r = lax.broadcasted_iota(jnp.int32, (3 * LANES, LANES), 0) % LANES
    c = lax.broadcasted_iota(jnp.int32, (3 * LANES, LANES), 1)
    u = jnp.where(r <= c, 1.0, 0.0).astype(BF16)
    carry = jnp.zeros((N_FOX_HEADS, 1), F32)
    for blk in range(seq // LANES):
        cols = slice(blk * LANES, (blk + 1) * LANES)
        y = jnp.dot(_split_cat(x_ref[0, :, cols], 3), u, preferred_element_type=F32) + carry
        o_ref[0, :, cols] = y * LOG2E
        carry = y[:, LANES - 1:LANES]


def _cumsum(lft):
    n_seq, _, seq = lft.shape
    spec = pl.BlockSpec((1, N_FOX_HEADS, seq), lambda b: (b, 0, 0))
    return pl.pallas_call(
        functools.partial(_cumsum_body, seq=seq),
        grid=(n_seq,), in_specs=[spec], out_specs=spec,
        out_shape=jax.ShapeDtypeStruct(lft.shape, F32),
        compiler_params=_params(("parallel",)),
        name="logf_cumsum",
    )(lft)


def _first_head_lanes(rows):
    return lax.broadcasted_iota(jnp.int32, (rows, LANES), 1) < HEAD_DIM


def _store_masked_q(q_ref, qm_ref, tq):
    first = _first_head_lanes(tq)
    qf = q_ref[...].astype(F32)
    qm_ref[0:tq, :] = jnp.where(first, qf, 0.0).astype(BF16)
    qm_ref[tq:2 * tq, :] = jnp.where(first, 0.0, qf).astype(BF16)


def _pair_head_norm(o0, o1, g):
    first = _first_head_lanes(o0.shape[0])
    o = jnp.where(first, o0, o1)
    o2 = o * o
    ss0 = jnp.sum(jnp.where(first, o2, 0.0), axis=-1, keepdims=True)
    ss1 = jnp.sum(jnp.where(first, 0.0, o2), axis=-1, keepdims=True)
    ms = jnp.where(first, ss0, ss1) * (1.0 / HEAD_DIM)
    return o * lax.rsqrt(ms + EPS) * g


def _key_chunk(ref, start):
    return ref[0, :, pl.ds(pl.multiple_of(start, LANES), LANES)]


def _diagonal_mask(rows, cmp):
    r = lax.broadcasted_iota(jnp.int32, (rows, LANES), 0)
    c = lax.broadcasted_iota(jnp.int32, (rows, LANES), 1)
    return (r >= LANES) | cmp(c, r)


def _sb_attn_body(q_ref, kt_ref, vt_ref, g_ref, o_ref, qm_ref, acc_ref, off_ref, *, tq):
    i = pl.program_id(2)
    n_groups = tq // LANES
    _store_masked_q(q_ref, qm_ref, tq)
    acc_ref[...] = jnp.zeros(acc_ref.shape, F32)
    off_ref[...] = jnp.zeros(off_ref.shape, F32)
    neg_suffix = _suffix_matrix(2, -1.0, inclusive=True)

    def process(units):
        stage = []
        for rows, kt, _, _ in units:
            z = jnp.dot(qm_ref[rows, :], kt, preferred_element_type=F32)
            stage.append((z, _softplus2(z)))
        sums = []
        for (_, sp), (_, _, _, mask) in zip(stage, units):
            if mask is not None:
                sp = jnp.where(mask, sp, 0.0)
            sums.append(jnp.dot(_split_cat(sp, 2), neg_suffix, preferred_element_type=F32))
        for (z, _), st, (rows, _, vt, mask) in zip(stage, sums, units):
            off = off_ref[rows, :]
            w = jnp.exp2(z + st[:, :LANES] + off)
            if mask is not None:
                w = jnp.where(mask, w, 0.0)
            acc_ref[rows, :] += lax.dot_general(w.astype(BF16), vt, _NT,
                                                preferred_element_type=F32)
            off_ref[rows, :] = off + st[:, LANES:]

    base = i * tq
    units = []
    for c in reversed(range(n_groups)):
        kt = _key_chunk(kt_ref, base + c * LANES)
        vt = _key_chunk(vt_ref, base + c * LANES)
        mask = _diagonal_mask(tq - c * LANES, lambda col, row: col < row)
        units += [(slice(h * tq + c * LANES, (h + 1) * tq), kt, vt, mask)
                  for h in range(HEADS_PER_BLOCK)]
    process(units)

    def earlier(jj, carry):
        keys = pl.ds(pl.multiple_of(base - (jj + 1) * tq, tq), tq)
        ktw = kt_ref[0, :, keys]
        vtw = vt_ref[0, :, keys]
        heads = [slice(h * tq, (h + 1) * tq) for h in range(HEADS_PER_BLOCK)]
        stage = []
        for rows in heads:
            z = jnp.dot(qm_ref[rows, :], ktw, preferred_element_type=F32)
            stage.append((z, _softplus2(z)))
        sums = [[jnp.dot(_split_cat(sp[:, c * LANES:(c + 1) * LANES], 2), neg_suffix,
                         preferred_element_type=F32) for c in range(n_groups)]
                for _, sp in stage]
        for rows, (z, _), sts in zip(heads, stage, sums):
            off = off_ref[rows, :]
            ws = [None] * n_groups
            for c in reversed(range(n_groups)):
                st = sts[c]
                ws[c] = jnp.exp2(z[:, c * LANES:(c + 1) * LANES] + st[:, :LANES]
                                 + off).astype(BF16)
                off = off + st[:, LANES:]
            off_ref[rows, :] = off
            acc_ref[rows, :] += lax.dot_general(jnp.concatenate(ws, axis=1), vtw, _NT,
                                                preferred_element_type=F32)
        return carry

    lax.fori_loop(0, i, earlier, 0)
    y = _pair_head_norm(acc_ref[0:tq, :], acc_ref[tq:2 * tq, :], g_ref[...])
    o_ref[...] = y.astype(BF16)


def _fox_attn_body(q_ref, kt_ref, vt_ref, cq_ref, ck_ref, g_ref, o_ref,
                   qm_ref, acc_ref, m_ref, cqs_ref, *, tq):
    hp = pl.program_id(1)
    i = pl.program_id(2)
    n_groups = tq // LANES
    _store_masked_q(q_ref, qm_ref, tq)
    cq_all = cq_ref[...]
    head_lane = lax.broadcasted_iota(jnp.int32, cq_all.shape, 1)
    for h in range(HEADS_PER_BLOCK):
        cqs_ref[h * tq:(h + 1) * tq, :] = jnp.sum(
            jnp.where(head_lane == HEADS_PER_BLOCK * hp + h, cq_all, 0.0), axis=-1, keepdims=True)
    acc_ref[...] = jnp.zeros(acc_ref.shape, F32)
    m_ref[...] = jnp.full(m_ref.shape, NEG_BIG, F32)
    ones = jnp.ones((LANES, tq), BF16)
    heads = [slice(h * tq, (h + 1) * tq) for h in range(HEADS_PER_BLOCK)]

    def key_tile(j, mask):
        keys = pl.ds(pl.multiple_of(j * tq, tq), tq)
        ktw = kt_ref[0, :, keys]
        vt1 = jnp.concatenate([vt_ref[0, :, keys], ones], axis=0)
        scores = []
        for h, rows in enumerate(heads):
            z = (jnp.dot(qm_ref[rows, :], ktw, preferred_element_type=F32)
                 + (cqs_ref[rows, :] - ck_ref[0, h, j]))
            scores.append(z if mask is None else jnp.where(mask, z, NEG_BIG))
        for rows, z in zip(heads, scores):
            m_old = m_ref[rows, :]
            m_new = jnp.maximum(m_old, jnp.max(z, axis=-1, keepdims=True))
            pv = lax.dot_general(jnp.exp2(z - m_new).astype(BF16), vt1, _NT,
                                 preferred_element_type=F32)
            acc_ref[rows, :] = acc_ref[rows, :] * jnp.exp2(m_old - m_new) + pv
            m_ref[rows, :] = m_new

    upto = (lax.broadcasted_iota(jnp.int32, (tq, tq), 1)
            <= lax.broadcasted_iota(jnp.int32, (tq, tq), 0))
    key_tile(i, upto)

    def earlier(jj, carry):
        key_tile(i - 1 - jj, None)
        return carry

    lax.fori_loop(0, i, earlier, 0)
    a0 = acc_ref[0:tq, :]
    a1 = acc_ref[tq:2 * tq, :]
    y = _pair_head_norm(a0[:, :LANES] / a0[:, LANES:], a1[:, :LANES] / a1[:, LANES:], g_ref[...])
    o_ref[...] = y.astype(BF16)


def _prompt_attention(q, kbt, vbt, cq, ck, g_sb, g_fox):
    n_seq, _, seq = kbt.shape
    tq = ATTN_TILE
    nq = seq // tq
    n_blk = SB_WIDTH // LANES
    rows = HEADS_PER_BLOCK * tq
    grid = (n_seq, n_blk, nq)
    out = jax.ShapeDtypeStruct((n_seq * seq, SB_WIDTH), BF16)
    sem = ("parallel", "parallel", "arbitrary")

    def specs(off):
        qs = pl.BlockSpec((tq, LANES), lambda b, hp, i: (b * nq + i, hp + off))
        kvs = pl.BlockSpec((1, LANES, seq), lambda b, hp, i: (b, hp + off, 0))
        return qs, kvs

    gain = pl.BlockSpec((1, LANES), lambda b, hp, i: (0, hp))
    o_spec = pl.BlockSpec((tq, LANES), lambda b, hp, i: (b * nq + i, hp))
    qm = pltpu.VMEM((rows, LANES), BF16)

    qs, kvs = specs(0)
    o_sb = pl.pallas_call(
        functools.partial(_sb_attn_body, tq=tq), grid=grid,
        in_specs=[qs, kvs, kvs, gain], out_specs=o_spec, out_shape=out,
        scratch_shapes=[qm, pltpu.VMEM((rows, LANES), F32), pltpu.VMEM((rows, LANES), F32)],
        compiler_params=_params(sem), name="sb_attention",
    )(q, kbt, vbt, g_sb)

    qs, kvs = specs(n_blk)
    cq_spec = pl.BlockSpec((tq, N_FOX_HEADS), lambda b, hp, i: (b * nq + i, 0))
    ck_spec = pl.BlockSpec((1, HEADS_PER_BLOCK, nq, 1, tq), lambda b, hp, i: (b, hp, 0, 0, 0))
    o_fox = pl.pallas_call(
        functools.partial(_fox_attn_body, tq=tq), grid=grid,
        in_specs=[qs, kvs, kvs, cq_spec, ck_spec, gain], out_specs=o_spec, out_shape=out,
        scratch_shapes=[qm, pltpu.VMEM((rows, 2 * LANES), F32),
                        pltpu.VMEM((rows, 1), F32), pltpu.VMEM((rows, 1), F32)],
        compiler_params=_params(sem), name="fox_attention",
    )(q, kbt, vbt, cq, ck, g_fox)
    return o_sb, o_fox


def _paged_attn_body(pt_ref, q_ref, kn_ref, vn_ref, lfn_ref, g_ref, *refs, npg):
    del pt_ref
    k_refs = refs[0:npg]
    v_refs = refs[npg:2 * npg]
    lf_refs = refs[2 * npg:3 * npg]
    o_ref = refs[3 * npg]
    acc_ref, m_ref, l_ref, osb_ref, ofx_ref = refs[3 * npg + 1:]
    step = pl.program_id(1)

    row = lax.broadcasted_iota(jnp.int32, (N_HEADS, D_MODEL), 0)
    col = lax.broadcasted_iota(jnp.int32, (N_HEADS, D_MODEL), 1)
    own = (col >= row * HEAD_DIM) & (col < (row + 1) * HEAD_DIM)
    qf = jnp.where(own, q_ref[0].astype(F32), 0.0)
    qmat = qf.astype(BF16)
    is_fox = lax.broadcasted_iota(jnp.int32, (N_HEADS, 1), 0) >= N_SB_HEADS

    @pl.when(step == 0)
    def _():
        z_self = jnp.sum(qf * kn_ref[0], axis=-1, keepdims=True)
        m_ref[...] = z_self[N_SB_HEADS:]
        l_ref[...] = jnp.ones((N_FOX_HEADS, 1), F32)
        acc_ref[...] = jnp.where(is_fox, jnp.broadcast_to(vn_ref[0], (N_HEADS, D_MODEL)), 0.0)
        osb_ref[...] = jnp.zeros((N_SB_HEADS, LANES), F32)
        ofx_ref[...] = jnp.broadcast_to(lfn_ref[0], (N_FOX_HEADS, LANES))

    kcat = jnp.concatenate([k_refs[p][0, 0].astype(BF16) for p in range(npg)], axis=1)
    z = jnp.dot(qmat, kcat, preferred_element_type=F32)
    z_sb = z[:N_SB_HEADS]
    sp = _softplus(z_sb)
    logsig = z_sb - sp
    neg_suffix = _suffix_matrix(2, -1.0)
    suffix = _suffix_matrix(3, 1.0)
    off_sb = osb_ref[...]
    off_fx = ofx_ref[...]
    w_sb = [None] * npg
    bias = [None] * npg
    for p in reversed(range(npg)):
        cols = slice(p * LANES, (p + 1) * LANES)
        st = jnp.dot(_split_cat(sp[:, cols], 2), neg_suffix, preferred_element_type=F32)
        w_sb[p] = jnp.exp(logsig[:, cols] + st[:, :LANES] + off_sb)
        off_sb = off_sb + st[:, LANES:]
        st = jnp.dot(_split_cat(lf_refs[p][0, 0], 3), suffix, preferred_element_type=F32)
        bias[p] = st[:, :LANES] + off_fx
        off_fx = off_fx + st[:, LANES:]
    osb_ref[...] = off_sb
    ofx_ref[...] = off_fx

    logit = z[N_SB_HEADS:] + jnp.concatenate(bias, axis=1)
    m_old = m_ref[...]
    m_new = jnp.maximum(m_old, jnp.max(logit, axis=-1, keepdims=True))
    alpha = jnp.exp(m_old - m_new)
    pr = jnp.exp(logit - m_new)
    l_ref[...] = alpha * l_ref[...] + jnp.sum(pr, axis=-1, keepdims=True)
    m_ref[...] = m_new

    w = jnp.concatenate([jnp.concatenate(w_sb, axis=1), pr], axis=0).astype(BF16)
    vcat = jnp.concatenate([v_refs[p][0, 0].astype(BF16) for p in range(npg)], axis=1)
    pv = lax.dot_general(w, vcat, _NT, preferred_element_type=F32)
    rescale = jnp.concatenate([jnp.ones((N_SB_HEADS, 1), F32), alpha], axis=0)
    acc_ref[...] = acc_ref[...] * rescale + pv

    @pl.when(step == pl.num_programs(1) - 1)
    def _():
        denom = jnp.concatenate([jnp.ones((N_SB_HEADS, 1), F32), l_ref[...]], axis=0)
        o = jnp.where(own, acc_ref[...] / denom, 0.0)
        ms = jnp.sum(o * o, axis=-1, keepdims=True) * (1.0 / HEAD_DIM)
        y = o * lax.rsqrt(ms + EPS)
        o_ref[0] = jnp.sum(y, axis=0, keepdims=True) * g_ref[...]


def _paged_attention(layer, page_table, q, k_new, v_new, lf_new, gains,
                     cache_kt, cache_vt, cache_lft):
    n_db, n_pages = page_table.shape
    npg = PAGES_PER_STEP
    n_steps = n_pages // npg
    row3 = lambda b, s, pt: (b, 0, 0)

    def page(p, shape):
        return pl.BlockSpec(
            (1, 1) + shape,
            lambda b, s, pt: (layer, pt[b, (n_steps - 1 - s) * npg + p], 0, 0))

    in_specs = [pl.BlockSpec((1, 1, D_MODEL), row3), pl.BlockSpec((1, 1, D_MODEL), row3),
                pl.BlockSpec((1, 1, D_MODEL), row3), pl.BlockSpec((1, N_FOX_HEADS, 1), row3),
                pl.BlockSpec((1, D_MODEL), lambda b, s, pt: (0, 0))]
    in_specs += [page(p, (D_MODEL, PAGE_SIZE)) for p in range(npg)]
    in_specs += [page(p, (D_MODEL, PAGE_SIZE)) for p in range(npg)]
    in_specs += [page(p, (N_FOX_HEADS, PAGE_SIZE)) for p in range(npg)]
    grid_spec = pltpu.PrefetchScalarGridSpec(
        num_scalar_prefetch=1, grid=(n_db, n_steps), in_specs=in_specs,
        out_specs=pl.BlockSpec((1, 1, D_MODEL), row3),
        scratch_shapes=[pltpu.VMEM((N_HEADS, D_MODEL), F32),
                        pltpu.VMEM((N_FOX_HEADS, 1), F32), pltpu.VMEM((N_FOX_HEADS, 1), F32),
                        pltpu.VMEM((N_SB_HEADS, LANES), F32), pltpu.VMEM((N_FOX_HEADS, LANES), F32)])
    return pl.pallas_call(
        functools.partial(_paged_attn_body, npg=npg),
        grid_spec=grid_spec,
        out_shape=jax.ShapeDtypeStruct((n_db, 1, D_MODEL), F32),
        compiler_params=_params(("parallel", "arbitrary")),
        name="paged_attention",
    )(page_table, q, k_new, v_new, lf_new, gains,
      *([cache_kt] * npg), *([cache_vt] * npg), *([cache_lft] * npg))


def _ffn_body(*refs, ts, tiles_per_seq, decode, final):
    fc = FF_CHUNK
    if decode:
        (x_ref, msb_ref, mfx_ref, p_ref, h0_ref, h1_ref, wo_ref, nf_ref, wup_ref, wc_ref, bc_ref,
         wd_ref, npl_ref, wg_ref, wp_ref, nfin_ref, out_ref, up_ref, y_ref) = refs
    else:
        (x_ref, msb_ref, mfx_ref, p_ref, wo_ref, nf_ref, wup_ref, wc_ref, bc_ref,
         wd_ref, npl_ref, wg_ref, wp_ref, nfin_ref, out_ref, tail_ref,
         y_ref, carry_ref, sh_ref) = refs

        @pl.when(pl.program_id(0) % tiles_per_seq == 0)
        def _():
            carry_ref[...] = jnp.zeros(carry_ref.shape, F32)

    x1 = (x_ref[...]
          + jnp.dot(msb_ref[...], wo_ref[0:SB_WIDTH, :], preferred_element_type=F32)
          + jnp.dot(mfx_ref[...], wo_ref[SB_WIDTH:, :], preferred_element_type=F32))
    h2 = _rms(x1, nf_ref[...]).astype(BF16)
    n_chunks = D_FF // fc

    def up_halves(c):
        return [jnp.dot(h2, wup_ref[:, base + c * fc:base + (c + 1) * fc],
                        preferred_element_type=F32) for base in (0, D_FF)]

    ups = up_halves(0)
    for c in range(n_chunks):
        nxt = up_halves(c + 1) if c + 1 < n_chunks else None
        halves = []
        for k, base in enumerate((0, D_FF)):
            cols = slice(base + c * fc, base + (c + 1) * fc)
            up = ups[k]
            if decode:
                sh2 = h0_ref[:, cols]
                sh1 = h1_ref[:, cols]
                up_ref[:, cols] = up
            else:
                buf = sh_ref.at[(2 * c + k) % sh_ref.shape[0]]
                buf[SUBLANES - 2:SUBLANES, :] = carry_ref[:, cols]
                buf[SUBLANES:SUBLANES + ts, :] = up
                sh1 = buf[SUBLANES - 1:SUBLANES - 1 + ts, :]
                sh2 = buf[SUBLANES - 2:SUBLANES - 2 + ts, :]
                carry_ref[:, cols] = up[ts - 2:ts, :]
                tail_ref[0, :, cols] = up[ts - 2:ts, :]
            halves.append(bc_ref[:, cols] + wc_ref[0:1, cols] * sh2
                          + wc_ref[1:2, cols] * sh1 + wc_ref[2:3, cols] * up)
        a, g = halves
        y_ref[:, c * fc:(c + 1) * fc] = (g * _sigmoid(g) * a).astype(BF16)
        ups = nxt
    x2 = x1 + jnp.dot(y_ref[...], wd_ref[...], preferred_element_type=F32)
    gate = _sigmoid(jnp.dot(_rms(x2, npl_ref[...]).astype(BF16), wg_ref[...],
                            preferred_element_type=F32))
    x3 = x2 + gate * jnp.dot(p_ref[...].astype(BF16), wp_ref[...], preferred_element_type=F32)
    out_ref[...] = _rms(x3, nfin_ref[...]) if final else x3


def _ffn(x, m_sb, m_fox, p, hist, w, nfin, *, ts, seq, final):
    t = x.shape[0]
    decode = hist is not None
    row = lambda i: (i, 0)
    tok = lambda width: pl.BlockSpec((ts, width), row)
    acts = [x, m_sb, m_fox, p]
    act_specs = [tok(D_MODEL), tok(SB_WIDTH), tok(SB_WIDTH), tok(PLE_DIM)]
    if decode:
        acts += list(hist)
        act_specs += [tok(2 * D_FF), tok(2 * D_FF)]
        out_shape = [jax.ShapeDtypeStruct((t, D_MODEL), F32),
                     jax.ShapeDtypeStruct((t, 2 * D_FF), F32)]
        out_specs = [tok(D_MODEL), tok(2 * D_FF)]
        scratch = [pltpu.VMEM((ts, D_FF), BF16)]
        tiles_per_seq = 1
    else:
        tiles_per_seq = seq // ts
        out_shape = [jax.ShapeDtypeStruct((t, D_MODEL), F32),
                     jax.ShapeDtypeStruct((t // seq, CONV_W - 1, 2 * D_FF), F32)]
        out_specs = [tok(D_MODEL),
                     pl.BlockSpec((1, CONV_W - 1, 2 * D_FF), lambda i: (i // tiles_per_seq, 0, 0))]
        scratch = [pltpu.VMEM((ts, D_FF), BF16), pltpu.VMEM((CONV_W - 1, 2 * D_FF), F32),
                   pltpu.VMEM((4, ts + SUBLANES, FF_CHUNK), F32)]
    weights = [w["w_o"], w["norm_ffn"], w["w_up"], w["w_conv"], w["b_conv"], w["w_down"],
               w["norm_ple"], w["w_ple_gate"], w["w_ple_proj"], nfin]
    return pl.pallas_call(
        functools.partial(_ffn_body, ts=ts, tiles_per_seq=tiles_per_seq, decode=decode,
                          final=final),
        grid=(t // ts,),
        in_specs=act_specs + [_resident()] * len(weights),
        out_specs=out_specs, out_shape=out_shape, scratch_shapes=scratch,
        compiler_params=_params(("arbitrary",)),
        name="ffn_decode" if decode else "ffn_prompt",
    )(*acts, *weights)


def kernel(x_prompt, x_sample, cache_k, cache_v, cache_logf, state_conv, page_table, p_prompt,
           p_sample, norm_attn, w_in, b_f, g_sb, g_fox, w_o, norm_ffn, w_up, w_conv, b_conv,
           w_down, norm_ple, w_ple_gate, w_ple_proj, norm_final):
    n_b, seq, _ = x_prompt.shape
    n_db = x_sample.shape[0]
    depth = w_in.shape[0]
    n_pool = cache_k.shape[1]
    t_p = n_b * seq

    h_p = x_prompt.reshape(t_p, D_MODEL)
    h_s = x_sample.reshape(n_db, D_MODEL)
    ck_pages = jnp.transpose(cache_k, (0, 1, 3, 4, 2)).reshape(depth, n_pool, D_MODEL, PAGE_SIZE)
    cv_pages = jnp.transpose(cache_v, (0, 1, 3, 4, 2)).reshape(depth, n_pool, D_MODEL, PAGE_SIZE)
    clf_pages = jnp.swapaxes(cache_logf, 2, 3)
    nfin = norm_final.reshape(1, D_MODEL)

    kt_all = vt_all = None
    outs = {name: [] for name in ("lp", "cp", "ks", "vs", "ls", "cs")}
    for i in range(depth):
        last = i == depth - 1
        wq = w_in[i, :, :D_MODEL].astype(BF16)
        wkt = w_in[i, :, D_MODEL:2 * D_MODEL].T.astype(BF16)
        wvt = w_in[i, :, 2 * D_MODEL:3 * D_MODEL].T.astype(BF16)
        wft = w_in[i, :, 3 * D_MODEL:].T.astype(BF16)
        bf = b_f[i].reshape(N_FOX_HEADS, 1)
        na = norm_attn[i].reshape(1, D_MODEL)
        gsb = g_sb[i].reshape(1, SB_WIDTH)
        gfx = g_fox[i].reshape(1, SB_WIDTH)
        wl = {
            "w_o": w_o[i].astype(BF16), "norm_ffn": norm_ffn[i].reshape(1, D_MODEL),
            "w_up": w_up[i].astype(BF16), "w_conv": w_conv[i],
            "b_conv": b_conv[i].reshape(1, 2 * D_FF), "w_down": w_down[i].astype(BF16),
            "norm_ple": norm_ple[i].reshape(1, D_MODEL),
            "w_ple_gate": w_ple_gate[i].astype(BF16), "w_ple_proj": w_ple_proj[i].astype(BF16),
        }

        q, kt_all, vt_all, kbt, vbt, lft = _inproj_prompt(
            i, depth, h_p, na, wq, wkt, wvt, wft, bf, kt_all, vt_all, n_b, seq)
        ct = _cumsum(lft)
        cq = jnp.swapaxes(ct, 1, 2).reshape(t_p, N_FOX_HEADS)
        ck = ct.reshape(n_b, N_FOX_HEADS, seq // ATTN_TILE, 1, ATTN_TILE)
        o_sb, o_fox = _prompt_attention(q, kbt, vbt, cq, ck, gsb, gfx)
        h_p, tail = _ffn(h_p, o_sb, o_fox, p_prompt[i].reshape(t_p, PLE_DIM), None, wl, nfin,
                         ts=TOKEN_TILE, seq=seq, final=last)
        outs["lp"].append(lft)
        outs["cp"].append(tail)

        q, k, v, lft = _inproj_rows(h_s, na, wq, wkt, wvt, wft, bf)
        mix = _paged_attention(
            i, page_table, q.reshape(n_db, 1, D_MODEL), k.reshape(n_db, 1, D_MODEL),
            v.reshape(n_db, 1, D_MODEL), lft.T.reshape(n_db, N_FOX_HEADS, 1),
            jnp.concatenate([gsb, gfx], axis=1), ck_pages, cv_pages, clf_pages)
        mix = mix.reshape(n_db, D_MODEL).astype(BF16)
        hist = (state_conv[i, :, 0], state_conv[i, :, 1])
        h_s, up = _ffn(h_s, mix[:, :SB_WIDTH], mix[:, SB_WIDTH:], p_sample[i].reshape(n_db, PLE_DIM),
                       hist, wl, nfin, ts=n_db, seq=1, final=last)
        outs["ks"].append(k.reshape(n_db, 1, N_HEADS, HEAD_DIM))
        outs["vs"].append(v.reshape(n_db, 1, N_HEADS, HEAD_DIM))
        outs["ls"].append(lft.T.reshape(n_db, 1, N_FOX_HEADS))
        outs["cs"].append(jnp.stack([hist[1], up], axis=1))

    def heads_last(a):
        return jnp.transpose(a.reshape(depth, n_b, N_HEADS, HEAD_DIM, seq), (0, 1, 4, 2, 3))

    return (h_p.reshape(n_b, seq, D_MODEL), h_s.reshape(n_db, 1, D_MODEL),
            heads_last(kt_all), heads_last(vt_all),
            jnp.swapaxes(jnp.stack(outs["lp"]), 2, 3),
            jnp.stack(outs["cp"]), jnp.stack(outs["ks"]), jnp.stack(outs["vs"]),
            jnp.stack(outs["ls"]), jnp.stack(outs["cs"]))
```
